```python
import math, functools
import jax, jax.numpy as jnp
from jax import lax
import numpy as np

D_MODEL = 2048
BATCH = 4
SEQ = 2048
DEPTH = 4
DEC_BATCH = 8
DEC_SEQ = 8
PAST_LEN = 16384
PAGE_SIZE = 128

BRANCH_W = D_MODEL // 2
N_BRANCH = 3
A_HEADS = 8
A_HD = BRANCH_W // (2 * A_HEADS)
A_VD = 2 * A_HD
A_Q_W = 2 * A_HEADS * A_HD
A_V_W = A_HEADS * A_VD
A_COLS = 2 * A_Q_W + A_V_W + BRANCH_W
Q_BLOCK = 128
ROPE_THETA = 10000.0
B_HD = 64
B_HEADS = BRANCH_W // B_HD
B_LORA = 64
B_SHIFT = 3 * BRANCH_W + 2 * B_LORA
B_COLS = B_SHIFT + BRANCH_W
B_GN_EPS = 64e-5
C_HEADS = 8
C_VD = BRANCH_W // C_HEADS
C_KD = C_VD // 2
C_QK_W = C_HEADS * C_KD
C_V_W = C_HEADS * C_VD
C_COLS = 2 * C_QK_W + C_V_W + BRANCH_W
C_CHUNK = 128
G_COLS = N_BRANCH * D_MODEL
IN_COLS = A_COLS + B_COLS + C_COLS + G_COLS
NORM_EPS = 1e-6
F32 = jnp.float32

kernel_name = "hybrid_diffattn_rwkv7_retention_step"


def rms_norm(x, w, eps=NORM_EPS):
    xf = x.astype(F32)
    y = xf * lax.rsqrt(jnp.mean(xf * xf, axis=-1, keepdims=True) + eps)
    return (y * w.astype(F32)).astype(x.dtype)


def rotate(x, pos, inv_freq):
    half = x.shape[-1] // 2
    ang = pos.astype(F32)[:, None] * inv_freq[None, :]
    cos = jnp.cos(ang)[:, None, :]
    sin = jnp.sin(ang)[:, None, :]
    xf = x.astype(F32)
    x1, x2 = xf[..., :half], xf[..., half:]
    return jnp.concatenate([x1 * cos - x2 * sin, x1 * sin + x2 * cos], axis=-1).astype(x.dtype)


def diff_combine(p, lam):
    bn, _, nq, nk = p.shape
    p = p.reshape(bn, A_HEADS, 2, nq, nk)
    return p[:, :, 0] - lam * p[:, :, 1]


def diff_attn_prompt(q, k, v, lam):
    bn, s = q.shape[:2]
    nb = s // Q_BLOCK
    scale = A_HD ** -0.5
    neg = jnp.finfo(F32).min
    kf, vf = k.astype(F32), v.astype(F32)
    qb = q.astype(F32).reshape(bn, nb, Q_BLOCK, 2 * A_HEADS, A_HD).swapaxes(0, 1)
    kpos = jnp.arange(s)

    def one_block(args):
        qi, i = args
        sc = jnp.einsum('bqnd,bknd->bnqk', qi, kf) * scale
        qpos = i * Q_BLOCK + jnp.arange(Q_BLOCK)
        sc = jnp.where(kpos[None, :] <= qpos[:, None], sc, neg)
        w = diff_combine(jax.nn.softmax(sc, axis=-1), lam)
        return jnp.einsum('bhqk,bkhd->bqhd', w, vf)

    out = lax.map(one_block, (qb, jnp.arange(nb)))
    return out.swapaxes(0, 1).reshape(bn, s, A_HEADS, A_VD)


def diff_attn_sample(q, k, v, lam, cache_k, cache_v, page_table, layer):
    bn, s = q.shape[:2]
    scale = A_HD ** -0.5
    neg = jnp.finfo(F32).min
    kp = cache_k[layer, page_table].reshape(bn, -1, 2 * A_HEADS, A_HD)
    vp = cache_v[layer, page_table].reshape(bn, -1, A_HEADS, A_VD)
    past = kp.shape[1]
    qf = q.astype(F32)
    s_past = jnp.einsum('bqnd,bknd->bnqk', qf, kp.astype(F32)) * scale
    s_new = jnp.einsum('bqnd,bknd->bnqk', qf, k.astype(F32)) * scale
    tri = jnp.arange(s)[None, :] <= jnp.arange(s)[:, None]
    s_new = jnp.where(tri, s_new, neg)
    p = jax.nn.softmax(jnp.concatenate([s_past, s_new], axis=-1), axis=-1)
    w = diff_combine(p, lam)
    return (jnp.einsum('bhqk,bkhd->bqhd', w[..., :past], vp.astype(F32))
            + jnp.einsum('bhqk,bkhd->bqhd', w[..., past:], v.astype(F32)))


def rwkv7_scan(r, decay, k, v, kk, a, state0):
    def step(st, inp):
        r_t, w_t, k_t, v_t, kk_t, a_t = inp
        sa = jnp.einsum('bhij,bhj->bhi', st, -kk_t)
        st = (st * w_t[:, :, None, :] + sa[..., None] * (kk_t * a_t)[:, :, None, :]
              + v_t[..., None] * k_t[:, :, None, :])
        return st, jnp.einsum('bhij,bhj->bhi', st, r_t)
    xs = tuple(t.swapaxes(0, 1) for t in (r, decay, k, v, kk, a))
    st, ys = lax.scan(step, state0, xs)
    return ys.swapaxes(0, 1), st


def rwkv7_branch(pb, shift0, state0, lw):
    bn, s, _ = pb.shape
    ps, g = pb[..., :B_SHIFT], pb[..., B_SHIFT:]
    prev = jnp.concatenate([shift0[:, None, :].astype(ps.dtype), ps[:, :-1]], axis=1)
    mixed = (ps + (prev - ps) * lw['b_mu']).astype(F32)
    r, wl, k, v, al = jnp.split(mixed, [BRANCH_W, BRANCH_W + B_LORA, 2 * BRANCH_W + B_LORA,
                                        3 * BRANCH_W + B_LORA], axis=-1)
    w_log = -jax.nn.softplus(-(lw['b_w0'] + jnp.tanh(wl) @ lw['b_w2'])) - 0.5
    decay = jnp.exp(-jnp.exp(w_log))
    a = jax.nn.sigmoid(lw['b_a0'] + al @ lw['b_a2'])
    heads = lambda t: t.reshape(bn, s, B_HEADS, B_HD)
    kk = heads(k * lw['b_kk'])
    kk = kk * lax.rsqrt(jnp.sum(kk * kk, axis=-1, keepdims=True) + 1e-12)
    k = k * (1.0 + (a - 1.0) * lw['b_ka'])
    r, decay, k, v, a = heads(r), heads(decay), heads(k), heads(v), heads(a)
    y, st = rwkv7_scan(r, decay, k, v, kk, a, state0.astype(F32))
    mu = jnp.mean(y, axis=-1, keepdims=True)
    var = jnp.mean(jnp.square(y - mu), axis=-1, keepdims=True)
    y = ((y - mu) * lax.rsqrt(var + B_GN_EPS) * lw['b_gn_w'].reshape(B_HEADS, B_HD)
         + lw['b_gn_b'].reshape(B_HEADS, B_HD))
    y = y + jnp.sum(r * k * lw['b_rk'], axis=-1, keepdims=True) * v
    out = y.reshape(bn, s, BRANCH_W) * jax.nn.silu(g.astype(F32))
    return out.astype(pb.dtype), st, ps[:, -1]


def retention_chunkwise(q, k, v, state0, chunk):
    bn, s, nh, _ = q.shape
    nc = s // chunk
    log_g = jnp.log(1.0 - 2.0 ** (-5.0 - jnp.arange(nh, dtype=F32)))
    idx = jnp.arange(chunk, dtype=F32)
    diff = idx[:, None] - idx[None, :]
    inner = jnp.exp(jnp.maximum(diff, 0.0)[None] * log_g[:, None, None]) * (diff >= 0)[None]
    xi = jnp.exp((idx[:, None] + 1.0) * log_g[None, :])
    zeta = jnp.exp((chunk - 1.0 - idx)[:, None] * log_g[None, :])
    g_chunk = jnp.exp(chunk * log_g)
    blocks = lambda t: t.reshape(bn, nc, chunk, nh, t.shape[-1]).swapaxes(0, 1)

    def step(st, inp):
        qc, kc, vc = inp
        sc = jnp.einsum('bnhd,bmhd->bhnm', qc, kc) * inner
        o = (jnp.einsum('bhnm,bmhe->bnhe', sc, vc)
             + jnp.einsum('bnhd,bhde->bnhe', qc, st) * xi[None, :, :, None])
        st = st * g_chunk[None, :, None, None] + jnp.einsum(
            'bmhd,bmhe->bhde', kc * zeta[None, :, :, None], vc)
        return st, o

    st, o = lax.scan(step, state0, (blocks(q), blocks(k), blocks(v)))
    return o.swapaxes(0, 1).reshape(bn, s, nh, v.shape[-1]), st


def retention_branch(pc, pos, state0):
    bn, s, _ = pc.shape
    q, k, v, g = jnp.split(pc, [C_QK_W, 2 * C_QK_W, 2 * C_QK_W + C_V_W], axis=-1)
    inv = ROPE_THETA ** (-jnp.linspace(0.0, 1.0, C_KD // 2, dtype=F32))
    q = rotate(q.reshape(bn, s, C_HEADS, C_KD).astype(F32), pos, inv)
    k = rotate(k.reshape(bn, s, C_HEADS, C_KD).astype(F32), pos, inv) * (C_KD ** -0.5)
    v = v.reshape(bn, s, C_HEADS, C_VD).astype(F32)
    chunk = C_CHUNK if s % C_CHUNK == 0 else s
    o, st = retention_chunkwise(q, k, v, state0.astype(F32), chunk)
    o = o * lax.rsqrt(jnp.mean(o * o, axis=-1, keepdims=True) + NORM_EPS)
    out = o.reshape(bn, s, BRANCH_W) * jax.nn.silu(g.astype(F32))
    return out.astype(pc.dtype), st


def trunk_layer(x, pos, layer_idx, attend, shift0, rwkv0, ret0, lw):
    bn, s, _ = x.shape
    h = rms_norm(x, lw['norm_w'])
    p = h @ lw['w_in']
    pa, pb, pc, pg = jnp.split(p, [A_COLS, A_COLS + B_COLS, A_COLS + B_COLS + C_COLS], axis=-1)
    qa, ka, va, ga = jnp.split(pa, [A_Q_W, 2 * A_Q_W, 2 * A_Q_W + A_V_W], axis=-1)
    inv = ROPE_THETA ** (-jnp.arange(A_HD // 2, dtype=F32) / (A_HD // 2))
    qa = rotate(rms_norm(qa.reshape(bn, s, 2 * A_HEADS, A_HD), lw['a_qnorm']), pos, inv)
    ka = rotate(rms_norm(ka.reshape(bn, s, 2 * A_HEADS, A_HD), lw['a_knorm']), pos, inv)
    va = va.reshape(bn, s, A_HEADS, A_VD)
    lam_init = 0.8 - 0.6 * math.exp(-0.3 * layer_idx)
    lv = lw['a_lambda'].astype(F32)
    lam = jnp.exp(jnp.sum(lv[0] * lv[1])) - jnp.exp(jnp.sum(lv[2] * lv[3])) + lam_init
    oa = attend(qa, ka, va, lam)
    oa = rms_norm(oa, lw['a_subln']) * (1.0 - lam_init)
    oa = (oa.reshape(bn, s, BRANCH_W) * jax.nn.silu(ga.astype(F32))).astype(x.dtype)
    ob, rwkv_st, shift_new = rwkv7_branch(pb, shift0, rwkv0, lw)
    oc, ret_st = retention_branch(pc, pos, ret0)
    branches = jnp.stack([oa, ob, oc], axis=2)
    proj = jnp.einsum('bsnc,ncd->bsnd', branches, lw['w_branch'])
    gates = jax.nn.sigmoid(pg.reshape(bn, s, N_BRANCH, D_MODEL).astype(F32))
    merged = jnp.sum(gates * proj.astype(F32), axis=2).astype(x.dtype)
    y = x + merged @ lw['w_out']
    return y, ka, va, rwkv_st, shift_new, ret_st


def setup_inputs(seed: int = 0) -> dict:
    key = jax.random.key(seed)
    ks = jax.random.split(key, 32)
    nrm = lambda i, shape, scale: scale * jax.random.normal(ks[i], shape, F32)
    n_pages = PAST_LEN // PAGE_SIZE
    n_pool = (5 * DEC_BATCH * n_pages + 3) // 4
    perm = jax.random.permutation(ks[3], n_pool)
    page_table = perm[:DEC_BATCH * n_pages].reshape(DEC_BATCH, n_pages).astype(jnp.int32)
    return {
        'x_prompt': nrm(0, (BATCH, SEQ, D_MODEL), 1.0),
        'x_sample': nrm(1, (DEC_BATCH, DEC_SEQ, D_MODEL), 1.0),
        'cache_k': nrm(2, (DEPTH, n_pool, PAGE_SIZE, 2 * A_HEADS, A_HD), 1.0),
        'cache_v': nrm(4, (DEPTH, n_pool, PAGE_SIZE, A_HEADS, A_VD), 1.0),
        'page_table': page_table,
        'state_rwkv': nrm(5, (DEPTH, DEC_BATCH, B_HEADS, B_HD, B_HD), 0.5),
        'state_shift': nrm(6, (DEPTH, DEC_BATCH, B_SHIFT), 1.0),
        'state_ret': nrm(7, (DEPTH, DEC_BATCH, C_HEADS, C_KD, C_VD), 1.0),
        'norm_w': 1.0 + nrm(8, (DEPTH, D_MODEL), 0.02),
        'w_in': nrm(9, (DEPTH, D_MODEL, IN_COLS), D_MODEL ** -0.5),
        'a_qnorm': 1.0 + nrm(10, (DEPTH, A_HD), 0.02),
        'a_knorm': 1.0 + nrm(11, (DEPTH, A_HD), 0.02),
        'a_lambda': nrm(12, (DEPTH, 4, A_HD), 0.1),
        'a_subln': 1.0 + nrm(13, (DEPTH, A_VD), 0.02),
        'b_mu': jax.random.uniform(ks[14], (DEPTH, B_SHIFT), F32),
        'b_w0': jax.random.uniform(ks[15], (DEPTH, BRANCH_W), F32, -6.0, -1.0),
        'b_w2': nrm(16, (DEPTH, B_LORA, BRANCH_W), 0.1),
        'b_a0': nrm(17, (DEPTH, BRANCH_W), 0.1),
        'b_a2': nrm(18, (DEPTH, B_LORA, BRANCH_W), 0.1),
        'b_kk': 1.0 + nrm(19, (DEPTH, BRANCH_W), 0.1),
        'b_ka': 1.0 + nrm(20, (DEPTH, BRANCH_W), 0.1),
        'b_rk': nrm(21, (DEPTH, B_HEADS, B_HD), 0.1),
        'b_gn_w': 1.0 + nrm(22, (DEPTH, BRANCH_W), 0.02),
        'b_gn_b': nrm(23, (DEPTH, BRANCH_W), 0.02),
        'w_branch': nrm(24, (DEPTH, N_BRANCH, BRANCH_W, D_MODEL), BRANCH_W ** -0.5),
        'w_out': nrm(25, (DEPTH, D_MODEL, D_MODEL), D_MODEL ** -0.5),
    }


def reference(x_prompt, x_sample, cache_k, cache_v, page_table, state_rwkv, state_shift, state_ret,
              norm_w, w_in, a_qnorm, a_knorm, a_lambda, a_subln, b_mu, b_w0, b_w2, b_a0, b_a2,
              b_kk, b_ka, b_rk, b_gn_w, b_gn_b, w_branch, w_out):
    bp, sp, _ = x_prompt.shape
    pos_p = jnp.arange(sp)
    pos_s = PAST_LEN + jnp.arange(x_sample.shape[1])
    shift0_p = jnp.zeros((bp, B_SHIFT), x_prompt.dtype)
    rwkv0_p = jnp.zeros((bp, B_HEADS, B_HD, B_HD), F32)
    ret0_p = jnp.zeros((bp, C_HEADS, C_KD, C_VD), F32)
    xp, xs = x_prompt, x_sample
    kp_l, vp_l, rwp_l, shp_l, rtp_l = [], [], [], [], []
    ks_l, vs_l, rws_l, shs_l, rts_l = [], [], [], [], []
    for l in range(DEPTH):
        lw = dict(norm_w=norm_w[l], w_in=w_in[l], a_qnorm=a_qnorm[l], a_knorm=a_knorm[l],
                  a_lambda=a_lambda[l], a_subln=a_subln[l], b_mu=b_mu[l], b_w0=b_w0[l],
                  b_w2=b_w2[l], b_a0=b_a0[l], b_a2=b_a2[l], b_kk=b_kk[l], b_ka=b_ka[l],
                  b_rk=b_rk[l], b_gn_w=b_gn_w[l], b_gn_b=b_gn_b[l], w_branch=w_branch[l],
                  w_out=w_out[l])
        xp, k_new, v_new, rw, sh, rt = trunk_layer(xp, pos_p, l, diff_attn_prompt,
                                                   shift0_p, rwkv0_p, ret0_p, lw)
        kp_l.append(k_new); vp_l.append(v_new); rwp_l.append(rw); shp_l.append(sh); rtp_l.append(rt)
        attend_s = functools.partial(diff_attn_sample, cache_k=cache_k, cache_v=cache_v,
                                     page_table=page_table, layer=l)
        xs, k_new, v_new, rw, sh, rt = trunk_layer(xs, pos_s, l, attend_s, state_shift[l],
                                                   state_rwkv[l], state_ret[l], lw)
        ks_l.append(k_new); vs_l.append(v_new); rws_l.append(rw); shs_l.append(sh); rts_l.append(rt)
    k_prompt = jnp.stack(kp_l)
    v_prompt = jnp.stack(vp_l)
    rwkv_prompt = jnp.stack(rwp_l)
    shift_prompt = jnp.stack(shp_l)
    ret_prompt = jnp.stack(rtp_l)
    k_sample = jnp.stack(ks_l)
    v_sample = jnp.stack(vs_l)
    rwkv_sample = jnp.stack(rws_l)
    shift_sample = jnp.stack(shs_l)
    ret_sample = jnp.stack(rts_l)
    return (xp, xs, k_prompt, v_prompt, rwkv_prompt, shift_prompt, ret_prompt,
            k_sample, v_sample, rwkv_sample, shift_sample, ret_sample)
```

```python
import functools
import math

import jax
import jax.numpy as jnp
from jax import lax
from jax.experimental import pallas as pl
from jax.experimental.pallas import tpu as pltpu

F32 = jnp.float32
BF16 = jnp.bfloat16

D_MODEL = 2048
DEPTH = 4
PAST_LEN = 16384
PAGE_SIZE = 128
BRANCH_W = 1024
A_HEADS = 8
A_HD = 64
B_HEADS = 16
B_LORA = 64
B_SHIFT = 3 * BRANCH_W + 2 * B_LORA
B_GN_EPS = 64e-5
C_HEADS = 8
C_KD = 64
ROPE_THETA = 10000.0
NORM_EPS = 1e-6
LANES = 128
HALF = 64
NEG_BIG = -1e30

COL_AQ, COL_AK, COL_AV, COL_AG = 0, 1024, 2048, 3072
COL_BR, COL_BK, COL_BV, COL_BG = 4096, 5120, 6144, 7168
COL_CQ, COL_CK, COL_CV, COL_CG = 8192, 8704, 9216, 10240
COL_GATE = 11264
N_MAIN = 17408
VMEM_LIMIT = 56 * 1024 * 1024


def _cparams(sem):
    return pltpu.CompilerParams(dimension_semantics=sem, vmem_limit_bytes=VMEM_LIMIT)


def _nt(a, b):
    return lax.dot_general(a, b, (((1,), (1,)), ((), ())), preferred_element_type=F32)


def _nn(a, b):
    return jnp.dot(a, b, preferred_element_type=F32)


def _silu(x):
    return x * (1.0 / (1.0 + jnp.exp(-x)))


def _sigmoid(x):
    return 1.0 / (1.0 + jnp.exp(-x))


def _first_head(shape):
    return lax.broadcasted_iota(jnp.int32, shape, len(shape) - 1) < HALF


def _seg_sum(x):
    first = _first_head(x.shape)
    zero = jnp.zeros_like(x)
    sa = jnp.sum(jnp.where(first, x, zero), axis=-1, keepdims=True)
    sb = jnp.sum(jnp.where(first, zero, x), axis=-1, keepdims=True)
    return jnp.where(first, sa, sb)


def _rope(x, cos, sin_signed):
    lane = lax.broadcasted_iota(jnp.int32, x.shape, 1)
    first_half = (lane % HALF) < (HALF // 2)
    partner = jnp.where(first_half, pltpu.roll(x, LANES - HALF // 2, 1), pltpu.roll(x, HALF // 2, 1))
    return x * cos + partner * sin_signed


def _rms_kernel(x_ref, w_ref, o_ref):
    x = x_ref[...]
    y = x * lax.rsqrt(jnp.mean(x * x, axis=-1, keepdims=True) + NORM_EPS)
    o_ref[...] = (y * w_ref[...]).astype(BF16)


def _rms_call(x2d, w, tm):
    m = x2d.shape[0]
    return pl.pallas_call(
        _rms_kernel,
        grid=(m // tm,),
        in_specs=[pl.BlockSpec((tm, D_MODEL), lambda i: (i, 0)), pl.BlockSpec((1, D_MODEL), lambda i: (0, 0))],
        out_specs=pl.BlockSpec((tm, D_MODEL), lambda i: (i, 0)),
        out_shape=jax.ShapeDtypeStruct((m, D_MODEL), BF16),
        compiler_params=_cparams(("parallel",)),
        name="rmsnorm",
    )(x2d, w)


def _mm_kernel(a_ref, b_ref, o_ref):
    o_ref[...] = _nn(a_ref[...], b_ref[...])


def _mm_call(a, b, tm, tn, name):
    m, k = a.shape
    n = b.shape[1]
    return pl.pallas_call(
        _mm_kernel,
        grid=(m // tm, n // tn),
        in_specs=[pl.BlockSpec((tm, k), lambda i, j: (i, 0)), pl.BlockSpec((k, tn), lambda i, j: (0, j))],
        out_specs=pl.BlockSpec((tm, tn), lambda i, j: (i, j)),
        out_shape=jax.ShapeDtypeStruct((m, n), F32),
        compiler_params=_cparams(("parallel", "arbitrary")),
        name=name,
    )(a, b)


def _qk_kernel(p_ref, nw_ref, cos_ref, sin_ref, *o_refs, scale):
    x = p_ref[...]
    ms = _seg_sum(x * x) * (1.0 / HALF)
    x = x * lax.rsqrt(ms + NORM_EPS) * nw_ref[...]
    y = _rope(x, cos_ref[...], sin_ref[...])
    if scale != 1.0:
        y = y * scale
    for o_ref in o_refs:
        o_ref[...] = y.astype(o_ref.dtype)


def _qk_call(p, col0, nw, cos, sin, tm, scale, out_dtypes, name):
    m = p.shape[0]
    ntab = cos.shape[0] // tm
    cb = col0 // LANES
    nblk = BRANCH_W // LANES
    outs = pl.pallas_call(
        functools.partial(_qk_kernel, scale=scale),
        grid=(m // tm, nblk),
        in_specs=[pl.BlockSpec((tm, LANES), lambda i, c: (i, cb + c)),
                  pl.BlockSpec((1, LANES), lambda i, c: (0, 0)),
                  pl.BlockSpec((tm, LANES), lambda i, c: (i % ntab, 0)),
                  pl.BlockSpec((tm, LANES), lambda i, c: (i % ntab, 0))],
        out_specs=[pl.BlockSpec((tm, LANES), lambda i, c: (i, c)) for _ in out_dtypes],
        out_shape=[jax.ShapeDtypeStruct((m, BRANCH_W), dt) for dt in out_dtypes],
        compiler_params=_cparams(("parallel", "parallel")),
        name=name,
    )(p, nw, cos, sin)
    return outs


def _lambda_value(lam_ref, lam_init):
    lv = lam_ref[...]
    s1 = jnp.sum(lv[0:1] * lv[1:2], axis=-1, keepdims=True)
    s2 = jnp.sum(lv[2:3] * lv[3:4], axis=-1, keepdims=True)
    return jnp.exp(s1) - jnp.exp(s2) + lam_init


def _subln_gate(o1, o2, lam, sub, ga, lam_init):
    a = o1 - lam * o2
    y = a * lax.rsqrt(jnp.mean(a * a, axis=-1, keepdims=True) + NORM_EPS) * sub
    return (y * (1.0 - lam_init)) * _silu(ga)


def _flash_kernel(q_ref, k_ref, v_ref, ga_ref, lam_ref, sub_ref, o_ref, q2_ref, m_ref, l_ref, acc_ref, *, tq, lam_init):
    qi = pl.program_id(2)
    ki = pl.program_id(3)

    @pl.when(ki == 0)
    def _():
        q = q_ref[...]
        first = _first_head(q.shape)
        zero = jnp.zeros_like(q)
        q2_ref[0:tq, :] = jnp.where(first, q, zero)
        q2_ref[tq:2 * tq, :] = jnp.where(first, zero, q)
        m_ref[...] = jnp.full(m_ref.shape, NEG_BIG, F32)
        l_ref[...] = jnp.zeros(l_ref.shape, F32)
        acc_ref[...] = jnp.zeros(acc_ref.shape, F32)

    @pl.when(ki <= qi)
    def _():
        s = _nt(q2_ref[...], k_ref[...])
        row = lax.broadcasted_iota(jnp.int32, s.shape, 0)
        col = lax.broadcasted_iota(jnp.int32, s.shape, 1)
        qpos = qi * tq + jnp.where(row >= tq, row - tq, row)
        s = jnp.where(ki * tq + col <= qpos, s, NEG_BIG)
        m_prev = m_ref[...]
        m_new = jnp.maximum(m_prev, jnp.max(s, axis=-1, keepdims=True))
        alpha = jnp.exp(m_prev - m_new)
        p = jnp.exp(s - m_new)
        l_ref[...] = alpha * l_ref[...] + jnp.sum(p, axis=-1, keepdims=True)
        acc_ref[...] = alpha * acc_ref[...] + _nn(p.astype(BF16), v_ref[...].astype(BF16))
        m_ref[...] = m_new

    @pl.when(ki == qi)
    def _():
        o = acc_ref[...] / l_ref[...]
        lam = _lambda_value(lam_ref, lam_init)
        y = _subln_gate(o[0:tq], o[tq:2 * tq], lam, sub_ref[...], ga_ref[...], lam_init)
        o_ref[...] = y.astype(BF16)


def _flash_call(q, k, p, a_lambda, subln, bn, s, tq, lam_init):
    nq = s // tq
    vb, gb = COL_AV // LANES, COL_AG // LANES
    return pl.pallas_call(
        functools.partial(_flash_kernel, tq=tq, lam_init=lam_init),
        grid=(bn, A_HEADS, nq, nq),
        in_specs=[pl.BlockSpec((tq, LANES), lambda b, h, i, j: (b * nq + i, h)),
                  pl.BlockSpec((tq, LANES), lambda b, h, i, j: (b * nq + jnp.minimum(i, j), h)),
                  pl.BlockSpec((tq, LANES), lambda b, h, i, j: (b * nq + jnp.minimum(i, j), vb + h)),
                  pl.BlockSpec((tq, LANES), lambda b, h, i, j: (b * nq + i, gb + h)),
                  pl.BlockSpec((4, A_HD), lambda b, h, i, j: (0, 0)),
                  pl.BlockSpec((1, LANES), lambda b, h, i, j: (0, 0))],
        out_specs=pl.BlockSpec((tq, LANES), lambda b, h, i, j: (b * nq + i, h)),
        out_shape=jax.ShapeDtypeStruct((bn * s, BRANCH_W), BF16),
        scratch_shapes=[pltpu.VMEM((2 * tq, LANES), BF16), pltpu.VMEM((2 * tq, 1), F32),
                        pltpu.VMEM((2 * tq, 1), F32), pltpu.VMEM((2 * tq, LANES), F32)],
        compiler_params=_cparams(("parallel", "parallel", "parallel", "arbitrary")),
        name="diff_attn_prompt",
    )(q, k, p, p, a_lambda, subln)


def _paged_kernel(pt_ref, q_ref, kc_ref, vc_ref, kn_ref, vn_ref, ga_ref, lam_ref, sub_ref, o_ref,
                  s_ref, m_ref, l_ref, acc_ref, *, n_pages, n_new, lam_init):
    del pt_ref
    pg = pl.program_id(1)
    nsub = 2 * A_HEADS

    @pl.when(pg == 0)
    def _():
        m_ref[...] = jnp.full(m_ref.shape, NEG_BIG, F32)
        l_ref[...] = jnp.zeros(l_ref.shape, F32)
        acc_ref[...] = jnp.zeros(acc_ref.shape, F32)

    def block(k_ref, v_ref, is_new):
        for n in range(nsub):
            kn = k_ref[pl.ds(n, PAGE_SIZE, stride=nsub), :].astype(BF16)
            s_ref[n_new * n:n_new * (n + 1), :] = _nt(q_ref[n], kn)
        s = s_ref[...]
        if is_new:
            row = lax.broadcasted_iota(jnp.int32, s.shape, 0)
            col = lax.broadcasted_iota(jnp.int32, s.shape, 1)
            s = jnp.where(col <= row % n_new, s, NEG_BIG)
        m_prev = m_ref[...]
        m_new = jnp.maximum(m_prev, jnp.max(s, axis=-1, keepdims=True))
        alpha = jnp.exp(m_prev - m_new)
        p = jnp.exp(s - m_new)
        l_ref[...] = alpha * l_ref[...] + jnp.sum(p, axis=-1, keepdims=True)
        m_ref[...] = m_new
        pb = p.astype(BF16)
        for h in range(A_HEADS):
            rows = slice(2 * n_new * h, 2 * n_new * (h + 1))
            vh = v_ref[pl.ds(h, PAGE_SIZE, stride=A_HEADS), :].astype(BF16)
            acc_ref[h] = alpha[rows] * acc_ref[h] + _nn(pb[rows], vh)

    @pl.when(pg < n_pages)
    def _():
        block(kc_ref, vc_ref, False)

    @pl.when(pg == n_pages)
    def _():
        block(kn_ref, vn_ref, True)
        lam = _lambda_value(lam_ref, lam_init)
        linv = 1.0 / l_ref[...]
        for h in range(A_HEADS):
            o = acc_ref[h] * linv[2 * n_new * h:2 * n_new * (h + 1)]
            cols = slice(LANES * h, LANES * (h + 1))
            y = _subln_gate(o[0:n_new], o[n_new:2 * n_new], lam, sub_ref[...], ga_ref[:, cols], lam_init)
            o_ref[:, cols] = y.astype(BF16)


def _paged_call(page_table, q4, cache_k2, cache_v2, knew, vnew, ga, a_lambda, subln, layer, lam_init):
    bn, n_pages = page_table.shape
    nsub = 2 * A_HEADS
    n_new = q4.shape[2]
    last = n_pages - 1
    grid_spec = pltpu.PrefetchScalarGridSpec(
        num_scalar_prefetch=1,
        grid=(bn, n_pages + 1),
        in_specs=[pl.BlockSpec((None, nsub, n_new, A_HD), lambda b, p, pt: (b, 0, 0, 0)),
                  pl.BlockSpec((None, None, PAGE_SIZE * nsub, A_HD),
                               lambda b, p, pt: (layer, pt[b, jnp.minimum(p, last)], 0, 0)),
                  pl.BlockSpec((None, None, PAGE_SIZE * A_HEADS, LANES),
                               lambda b, p, pt: (layer, pt[b, jnp.minimum(p, last)], 0, 0)),
                  pl.BlockSpec((None, PAGE_SIZE * nsub, A_HD), lambda b, p, pt: (b, 0, 0)),
                  pl.BlockSpec((None, PAGE_SIZE * A_HEADS, LANES), lambda b, p, pt: (b, 0, 0)),
                  pl.BlockSpec((None, n_new, BRANCH_W), lambda b, p, pt: (b, 0, 0)),
                  pl.BlockSpec((4, A_HD), lambda b, p, pt: (0, 0)),
                  pl.BlockSpec((1, LANES), lambda b, p, pt: (0, 0))],
        out_specs=pl.BlockSpec((None, n_new, BRANCH_W), lambda b, p, pt: (b, 0, 0)),
        scratch_shapes=[pltpu.VMEM((nsub * n_new, PAGE_SIZE), F32), pltpu.VMEM((nsub * n_new, 1), F32),
                        pltpu.VMEM((nsub * n_new, 1), F32), pltpu.VMEM((A_HEADS, 2 * n_new, LANES), F32)],
    )
    return pl.pallas_call(
        functools.partial(_paged_kernel, n_pages=n_pages, n_new=n_new, lam_init=lam_init),
        grid_spec=grid_spec,
        out_shape=jax.ShapeDtypeStruct((bn, n_new, BRANCH_W), BF16),
        compiler_params=_cparams(("parallel", "arbitrary")),
        name="diff_attn_sample",
    )(page_table, q4, cache_k2, cache_v2, knew, vnew, ga, a_lambda, subln)


def _shift_rows(x, prev_row):
    row = lax.broadcasted_iota(jnp.int32, x.shape, 0)
    return jnp.where(row == 0, prev_row, pltpu.roll(x, 1, 0))


def _cumsum_rows(x):
    n = x.shape[0]
    row = lax.broadcasted_iota(jnp.int32, x.shape, 0)
    d = 1
    while d < n:
        x = x + jnp.where(row >= d, pltpu.roll(x, d, 0), 0.0)
        d *= 2
    return x


def _bmm(a, b):
    return _nn(a.astype(BF16), b.astype(BF16))


def _rwkv_pair(s_old, at, rt, bt, kt, bh, kh, v, decay_l, ln):
    first = _first_head((1, LANES))
    row = lax.broadcasted_iota(jnp.int32, (ln, ln), 0)
    col = lax.broadcasted_iota(jnp.int32, (ln, ln), 1)
    row2 = lax.broadcasted_iota(jnp.int32, (ln, 2 * ln), 0)
    col2 = lax.broadcasted_iota(jnp.int32, (ln, 2 * ln), 1)
    col2 = jnp.where(col2 >= ln, col2 - ln, col2)
    strict = col < row
    incl2 = col2 <= row2
    strict_k = (lax.broadcasted_iota(jnp.int32, (ln, 2 * ln), 1) >= ln) & (col2 < row2)
    zero = jnp.zeros_like(at)
    lhs4 = jnp.concatenate([jnp.where(first, at, zero), jnp.where(first, zero, at),
                            jnp.where(first, rt, zero), jnp.where(first, zero, rt)], axis=0).astype(BF16)
    rhs = jnp.concatenate([bt, kt], axis=0).astype(BF16)
    mm = _nt(lhs4, rhs)
    eye = (row == col).astype(F32)
    vv = jnp.concatenate([v, v], axis=0).astype(BF16)
    g = _nt(jnp.concatenate([at, rt], axis=0).astype(BF16), s_old.astype(BF16))
    us = []
    for h in range(2):
        blk = mm[h * ln:(h + 1) * ln]
        x = jnp.where(strict, blk[:, 0:ln], 0.0)
        t = eye + x
        n = 1
        while 2 * n < ln:
            x = _bmm(x, x)
            t = t + _bmm(t, x)
            n *= 2
        x0 = _nn(jnp.where(strict_k, blk, 0.0).astype(BF16), vv)
        us.append(_bmm(t, g[0:ln] + x0))
    u = jnp.where(first, us[0], us[1])
    uv = jnp.concatenate([u, v], axis=0)
    uvb = uv.astype(BF16)
    ya = _nn(jnp.where(incl2, mm[2 * ln:3 * ln], 0.0).astype(BF16), uvb)
    yb = _nn(jnp.where(incl2, mm[3 * ln:4 * ln], 0.0).astype(BF16), uvb)
    y = g[ln:2 * ln] + jnp.where(first, ya, yb)
    upd = _nn(uv.T.astype(BF16), jnp.concatenate([bh, kh], axis=0).astype(BF16))
    ri = lax.broadcasted_iota(jnp.int32, (LANES, LANES), 0) < HALF
    ci = lax.broadcasted_iota(jnp.int32, (LANES, LANES), 1) < HALF
    s_new = s_old * decay_l + jnp.where(ri == ci, upd, 0.0)
    return y, s_new


def _rwkv_kernel(r_ref, k_ref, v_ref, g_ref, lo_ref, sr_ref, sk_ref, sv_ref, sl_ref, st0_ref,
                 mur_ref, muk_ref, muv_ref, mul_ref, w0_ref, a0_ref, kkw_ref, kaw_ref, rk_ref, gnw_ref, gnb_ref,
                 w2_ref, a2_ref, o_ref, st_ref, pr_ref, pk_ref, pv_ref, pl_ref, s_ref, *, ln, n_valid):
    c = pl.program_id(1)

    @pl.when(c == 0)
    def _():
        pr_ref[...] = sr_ref[...]
        pk_ref[...] = sk_ref[...]
        pv_ref[...] = sv_ref[...]
        pl_ref[...] = sl_ref[...]
        s_ref[...] = st0_ref[...]

    def mix(x_ref, prev_ref, mu_ref):
        x = x_ref[...]
        xm = x + (_shift_rows(x, prev_ref[...]) - x) * mu_ref[...]
        prev_ref[...] = x[n_valid - 1:n_valid]
        return xm

    r = mix(r_ref, pr_ref, mur_ref)
    k = mix(k_ref, pk_ref, muk_ref)
    v = mix(v_ref, pv_ref, muv_ref)
    lo = mix(lo_ref, pl_ref, mul_ref)
    w_pre = w0_ref[...] + _nn(jnp.tanh(lo).astype(BF16), w2_ref[...])
    neg = -w_pre
    softplus = jnp.maximum(neg, 0.0) + jnp.log(1.0 + jnp.exp(-jnp.abs(neg)))
    logw = -jnp.exp(-softplus - 0.5)
    a = _sigmoid(a0_ref[...] + _nn(lo.astype(BF16), a2_ref[...]))
    kk = k * kkw_ref[...]
    k2 = k * (1.0 + (a - 1.0) * kaw_ref[...])
    if n_valid < ln:
        live = lax.broadcasted_iota(jnp.int32, logw.shape, 0) < n_valid
        logw = jnp.where(live, logw, 0.0)
        kk = jnp.where(live, kk, 0.0)
        k2 = jnp.where(live, k2, 0.0)
    cum = _cumsum_rows(logw)
    e_in = jnp.exp(cum)
    e_ex = jnp.exp(cum - logw)
    e_inv = jnp.exp(-cum)
    c_last = cum[ln - 1:ln]
    e_last = jnp.exp(c_last - cum)
    decay_l = jnp.exp(c_last)
    rk_all = r * k2 * rk_ref[...]
    gate = _silu(g_ref[...])
    for pr in range(B_HEADS // 2):
        cols = slice(LANES * pr, LANES * (pr + 1))
        kkp = kk[:, cols]
        kkp = kkp * lax.rsqrt(_seg_sum(kkp * kkp) + 1e-12)
        b = kkp * a[:, cols]
        at = -kkp * e_ex[:, cols]
        rt = r[:, cols] * e_in[:, cols]
        bt = b * e_inv[:, cols]
        kt = k2[:, cols] * e_inv[:, cols]
        bh = b * e_last[:, cols]
        kh = k2[:, cols] * e_last[:, cols]
        vp = v[:, cols]
        y, s_new = _rwkv_pair(s_ref[pr], at, rt, bt, kt, bh, kh, vp, decay_l[:, cols], ln)
        s_ref[pr] = s_new
        mu = _seg_sum(y) * (1.0 / HALF)
        dy = y - mu
        var = _seg_sum(dy * dy) * (1.0 / HALF)
        yn = dy * lax.rsqrt(var + B_GN_EPS) * gnw_ref[:, cols] + gnb_ref[:, cols]
        yn = yn + _seg_sum(rk_all[:, cols]) * vp
        o_ref[:, cols] = (yn * gate[:, cols]).astype(BF16)

    @pl.when(c == pl.num_programs(1) - 1)
    def _():
        st_ref[...] = s_ref[...]


def _rwkv_call(p, lora, shift_r, shift_k, shift_v, shift_l, state_bd, prm, bn, s, ln, n_valid):
    nc = s // ln
    npair = B_HEADS // 2
    rb, kb, vb, gb = (c // BRANCH_W for c in (COL_BR, COL_BK, COL_BV, COL_BG))
    tok = lambda cb: pl.BlockSpec((ln, BRANCH_W), lambda b, c: (b * nc + c, cb))
    per_b = lambda w: pl.BlockSpec((None, 1, w), lambda b, c: (b, 0, 0))
    row = lambda w: pl.BlockSpec((1, w), lambda b, c: (0, 0))
    st_spec = pl.BlockSpec((None, npair, LANES, LANES), lambda b, c: (b, 0, 0, 0))
    return pl.pallas_call(
        functools.partial(_rwkv_kernel, ln=ln, n_valid=n_valid),
        grid=(bn, nc),
        in_specs=[tok(rb), tok(kb), tok(vb), tok(gb),
                  pl.BlockSpec((ln, LANES), lambda b, c: (b * nc + c, 0)),
                  per_b(BRANCH_W), per_b(BRANCH_W), per_b(BRANCH_W), per_b(LANES), st_spec,
                  row(BRANCH_W), row(BRANCH_W), row(BRANCH_W), row(LANES),
                  row(BRANCH_W), row(BRANCH_W), row(BRANCH_W), row(BRANCH_W), row(BRANCH_W), row(BRANCH_W),
                  row(BRANCH_W),
                  pl.BlockSpec((LANES, BRANCH_W), lambda b, c: (0, 0)),
                  pl.BlockSpec((LANES, BRANCH_W), lambda b, c: (0, 0))],
        out_specs=[pl.BlockSpec((ln, BRANCH_W), lambda b, c: (b * nc + c, 0)), st_spec],
        out_shape=[jax.ShapeDtypeStruct((bn * s, BRANCH_W), BF16),
                   jax.ShapeDtypeStruct((bn, npair, LANES, LANES), F32)],
        scratch_shapes=[pltpu.VMEM((1, BRANCH_W), F32), pltpu.VMEM((1, BRANCH_W), F32),
                        pltpu.VMEM((1, BRANCH_W), F32), pltpu.VMEM((1, LANES), F32),
                        pltpu.VMEM((npair, LANES, LANES), F32)],
        compiler_params=_cparams(("parallel", "arbitrary")),
        name="rwkv7_chunked",
    )(p, p, p, p, lora, shift_r, shift_k, shift_v, shift_l, state_bd,
      prm["mu_r"], prm["mu_k"], prm["mu_v"], prm["mu_l"], prm["w0"], prm["a0"], prm["kk"], prm["ka"], prm["rk"],
      prm["gn_w"], prm["gn_b"], prm["w2"], prm["a2"])


def _ret_kernel(q_ref, k_ref, v_ref, g_ref, cos_ref, sin_ref, st0_ref, o_ref, st_ref, s_ref, *, cn, n_valid):
    c = pl.program_id(1)

    @pl.when(c == 0)
    def _():
        s_ref[...] = st0_ref[...]

    cos = cos_ref[...]
    sin = sin_ref[...]
    rowc = lax.broadcasted_iota(jnp.int32, (cn, cn), 0)
    colc = lax.broadcasted_iota(jnp.int32, (cn, cn), 1)
    dist = (rowc - colc).astype(F32)
    causal = colc <= rowc
    n_idx = lax.broadcasted_iota(jnp.int32, (cn, 1), 0).astype(F32)
    first128 = _first_head((1, LANES))
    first256 = lax.broadcasted_iota(jnp.int32, (1, 2 * LANES), 1) < LANES
    ri = lax.broadcasted_iota(jnp.int32, (LANES, 2 * LANES), 0) < HALF
    ci = lax.broadcasted_iota(jnp.int32, (LANES, 2 * LANES), 1) < LANES
    live = lax.broadcasted_iota(jnp.int32, (cn, 1), 0) < n_valid
    for pr in range(C_HEADS // 2):
        lg = [math.log(1.0 - 2.0 ** (-5.0 - (2 * pr + j))) for j in range(2)]
        qc = slice(LANES * pr, LANES * (pr + 1))
        vc = slice(2 * LANES * pr, 2 * LANES * (pr + 1))
        q = _rope(q_ref[:, qc], cos, sin)
        k = _rope(k_ref[:, qc], cos, sin) * (C_KD ** -0.5)
        if n_valid < cn:
            k = jnp.where(live, k, 0.0)
        v = v_ref[:, vc]
        vb = v.astype(BF16)
        kb = k.astype(BF16)
        zero = jnp.zeros_like(q)
        outs = []
        for j in range(2):
            qj = jnp.where(first128, q, zero) if j == 0 else jnp.where(first128, zero, q)
            sc = _nt(qj.astype(BF16), kb)
            sc = sc * jnp.where(causal, jnp.exp(jnp.maximum(dist, 0.0) * lg[j]), 0.0)
            outs.append(_nn(sc.astype(BF16), vb[:, LANES * j:LANES * (j + 1)]))
        st = s_ref[pr]
        xi = jnp.where(first256, jnp.exp((n_idx + 1.0) * lg[0]), jnp.exp((n_idx + 1.0) * lg[1]))
        o = jnp.concatenate(outs, axis=1) + _nn(q.astype(BF16), st.astype(BF16)) * xi
        zeta = jnp.where(first128, jnp.exp((n_valid - 1.0 - n_idx) * lg[0]), jnp.exp((n_valid - 1.0 - n_idx) * lg[1]))
        upd = _nn((k * zeta).T.astype(BF16), vb)
        gch = jnp.where(first256, math.exp(n_valid * lg[0]), math.exp(n_valid * lg[1]))
        s_ref[pr] = st * gch + jnp.where(ri == ci, upd, 0.0)
        gate = _silu(g_ref[:, vc])
        for j in range(2):
            oj = o[:, LANES * j:LANES * (j + 1)]
            oj = oj * lax.rsqrt(jnp.mean(oj * oj, axis=-1, keepdims=True) + NORM_EPS)
            o_ref[:, 2 * LANES * pr + LANES * j:2 * LANES * pr + LANES * (j + 1)] = (
                oj * gate[:, LANES * j:LANES * (j + 1)]).astype(BF16)

    @pl.when(c == pl.num_programs(1) - 1)
    def _():
        st_ref[...] = s_ref[...]


def _ret_call(p, cos, sin, state_bd, bn, s, cn, n_valid):
    nc = s // cn
    npair = C_HEADS // 2
    ntab = cos.shape[0] // cn
    st_spec = pl.BlockSpec((None, npair, LANES, 2 * LANES), lambda b, c: (b, 0, 0, 0))
    return pl.pallas_call(
        functools.partial(_ret_kernel, cn=cn, n_valid=n_valid),
        grid=(bn, nc),
        in_specs=[pl.BlockSpec((cn, 512), lambda b, c: (b * nc + c, COL_CQ // 512)),
                  pl.BlockSpec((cn, 512), lambda b, c: (b * nc + c, COL_CK // 512)),
                  pl.BlockSpec((cn, BRANCH_W), lambda b, c: (b * nc + c, COL_CV // BRANCH_W)),
                  pl.BlockSpec((cn, BRANCH_W), lambda b, c: (b * nc + c, COL_CG // BRANCH_W)),
                  pl.BlockSpec((cn, LANES), lambda b, c: (c % ntab, 0)),
                  pl.BlockSpec((cn, LANES), lambda b, c: (c % ntab, 0)),
                  st_spec],
        out_specs=[pl.BlockSpec((cn, BRANCH_W), lambda b, c: (b * nc + c, 0)), st_spec],
        out_shape=[jax.ShapeDtypeStruct((bn * s, BRANCH_W), BF16),
                   jax.ShapeDtypeStruct((bn, npair, LANES, 2 * LANES), F32)],
        scratch_shapes=[pltpu.VMEM((npair, LANES, 2 * LANES), F32)],
        compiler_params=_cparams(("parallel", "arbitrary")),
        name="retention_chunkwise",
    )(p, p, p, p, cos, sin, state_bd)


def _merge_kernel(oa_ref, ob_ref, oc_ref, wb_ref, g0_ref, g1_ref, g2_ref, o_ref):
    acc = _sigmoid(g0_ref[...]) * _nn(oa_ref[...], wb_ref[0])
    acc = acc + _sigmoid(g1_ref[...]) * _nn(ob_ref[...], wb_ref[1])
    acc = acc + _sigmoid(g2_ref[...]) * _nn(oc_ref[...], wb_ref[2])
    o_ref[...] = acc.astype(BF16)


def _merge_call(oa, ob, oc, wb, p, tm, tn):
    m = oa.shape[0]
    nj = D_MODEL // tn
    act = pl.BlockSpec((tm, BRANCH_W), lambda i, j: (i, 0))
    gate = lambda n: pl.BlockSpec((tm, tn), lambda i, j: (i, (COL_GATE + n * D_MODEL) // tn + j))
    return pl.pallas_call(
        _merge_kernel,
        grid=(m // tm, nj),
        in_specs=[act, act, act, pl.BlockSpec((3, BRANCH_W, tn), lambda i, j: (0, 0, j)), gate(0), gate(1), gate(2)],
        out_specs=pl.BlockSpec((tm, tn), lambda i, j: (i, j)),
        out_shape=jax.ShapeDtypeStruct((m, D_MODEL), BF16),
        compiler_params=_cparams(("parallel", "arbitrary")),
        name="branch_merge",
    )(oa, ob, oc, wb, p, p, p)


def _out_kernel(a_ref, b_ref, x_ref, o_ref):
    o_ref[...] = x_ref[...] + _nn(a_ref[...], b_ref[...])


def _out_call(merged, w_out, x2d, tm, tn):
    m = merged.shape[0]
    return pl.pallas_call(
        _out_kernel,
        grid=(m // tm, D_MODEL // tn),
        in_specs=[pl.BlockSpec((tm, D_MODEL), lambda i, j: (i, 0)), pl.BlockSpec((D_MODEL, tn), lambda i, j: (0, j)),
                  pl.BlockSpec((tm, tn), lambda i, j: (i, j))],
        out_specs=pl.BlockSpec((tm, tn), lambda i, j: (i, j)),
        out_shape=jax.ShapeDtypeStruct((m, D_MODEL), F32),
        compiler_params=_cparams(("parallel", "arbitrary")),
        name="out_proj_residual",
    )(merged, w_out, x2d)


def _rope_tables(pos, inv_freq):
    ang = pos.astype(F32)[:, None] * inv_freq[None, :]
    cos = jnp.cos(ang)
    sin = jnp.sin(ang)
    cos = jnp.concatenate([cos, cos, cos, cos], axis=1)
    sin = jnp.concatenate([-sin, sin, -sin, sin], axis=1)
    return cos, sin


def _permute_w_in(w):
    b0 = 4096
    c0 = b0 + B_SHIFT + BRANCH_W
    g0 = c0 + 3072
    br = w[:, b0:b0 + 1024]
    bwl = w[:, b0 + 1024:b0 + 1088]
    bk = w[:, b0 + 1088:b0 + 2112]
    bv = w[:, b0 + 2112:b0 + 3136]
    bal = w[:, b0 + 3136:b0 + 3200]
    bg = w[:, b0 + 3200:b0 + 4224]
    main = jnp.concatenate([w[:, 0:b0], br, bk, bv, bg, w[:, c0:g0], w[:, g0:g0 + 3 * D_MODEL]], axis=1)
    lora = jnp.concatenate([bwl, bal], axis=1)
    return main.astype(BF16), lora.astype(BF16)


def _split_shift(sh):
    return (sh[..., 0:1024], sh[..., 1088:2112], sh[..., 2112:3136],
            jnp.concatenate([sh[..., 1024:1088], sh[..., 3136:3200]], axis=-1))


def _join_shift(r, k, v, lo):
    return jnp.concatenate([r, lo[..., 0:64], k, v, lo[..., 64:128]], axis=-1)


def _pairs_to_blockdiag(st):
    bn, nh, r, c = st.shape
    st = st.reshape(bn, nh // 2, 2, r, c)
    z = jnp.zeros_like(st[:, :, 0])
    top = jnp.concatenate([st[:, :, 0], z], axis=-1)
    bot = jnp.concatenate([z, st[:, :, 1]], axis=-1)
    return jnp.concatenate([top, bot], axis=-2)


def _blockdiag_to_pairs(bd):
    bn, npair, r2, c2 = bd.shape
    r, c = r2 // 2, c2 // 2
    return jnp.stack([bd[:, :, :r, :c], bd[:, :, r:, c:]], axis=2).reshape(bn, 2 * npair, r, c)


def _layer_params(l, w_main, w_lora, w_branch_bf, w_out_bf, norm_w, a_qnorm, a_knorm, a_lambda, a_subln, b_mu, b_w0,
                  b_w2, b_a0, b_a2, b_kk, b_ka, b_rk, b_gn_w, b_gn_b):
    mu_r, mu_k, mu_v, mu_l = _split_shift(b_mu[l])
    zeros = jnp.zeros((B_LORA, BRANCH_W), F32)
    row = lambda t: t.reshape(1, -1)
    return dict(
        w_main=w_main[l], w_lora=w_lora[l], w_branch=w_branch_bf[l], w_out=w_out_bf[l], norm_w=row(norm_w[l]),
        qn=row(jnp.tile(a_qnorm[l], 2)), kn=row(jnp.tile(a_knorm[l], 2)), a_lambda=a_lambda[l], subln=row(a_subln[l]),
        lam_init=0.8 - 0.6 * math.exp(-0.3 * l),
        rwkv=dict(mu_r=row(mu_r), mu_k=row(mu_k), mu_v=row(mu_v), mu_l=row(mu_l), w0=row(b_w0[l]), a0=row(b_a0[l]),
                  kk=row(b_kk[l]), ka=row(b_ka[l]), rk=row(b_rk[l]), gn_w=row(b_gn_w[l]), gn_b=row(b_gn_b[l]),
                  w2=jnp.concatenate([b_w2[l], zeros], axis=0).astype(BF16),
                  a2=jnp.concatenate([zeros, b_a2[l]], axis=0).astype(BF16)))


def _tiles(m):
    return min(m, 1024)


def _trunk_layer(x2d, bn, s, n_valid, lp, tabs_a, tabs_c, attend, shift0, rwkv0_bd, ret0_bd, chunk_b, chunk_c):
    m = bn * s
    tm = _tiles(m)
    h = _rms_call(x2d, lp["norm_w"], min(m, 512))
    p = _mm_call(h, lp["w_main"], tm, 1024, "in_proj")
    lora = _mm_call(h, lp["w_lora"], tm, LANES, "in_proj_lora")
    t_qk = min(tabs_a[0].shape[0], 512)
    (q_bf,) = _qk_call(p, COL_AQ, lp["qn"], tabs_a[0], tabs_a[1], t_qk, A_HD ** -0.5, (BF16,), "q_norm_rope")
    k_f32, k_bf = _qk_call(p, COL_AK, lp["kn"], tabs_a[0], tabs_a[1], t_qk, 1.0, (F32, BF16), "k_norm_rope")
    oa = attend(q_bf, k_f32, k_bf, p)
    sr, sk, sv, sl = shift0
    ob, rwkv_bd = _rwkv_call(p, lora, sr, sk, sv, sl, rwkv0_bd, lp["rwkv"], bn, s, chunk_b, min(n_valid, chunk_b))
    oc, ret_bd = _ret_call(p, tabs_c[0], tabs_c[1], ret0_bd, bn, s, chunk_c, min(n_valid, chunk_c))
    merged = _merge_call(oa, ob, oc, lp["w_branch"], p, min(m, 512), 512)
    y = _out_call(merged, lp["w_out"], x2d, tm, 1024)
    return y, k_f32, p, lora, rwkv_bd, ret_bd


def kernel(x_prompt, x_sample, cache_k, cache_v, page_table, state_rwkv, state_shift, state_ret, norm_w, w_in, a_qnorm, a_knorm, a_lambda, a_subln, b_mu, b_w0, b_w2, b_a0, b_a2, b_kk, b_ka, b_rk, b_gn_w, b_gn_b, w_branch, w_out):
    bp, sp, _ = x_prompt.shape
    bs, ss, _ = x_sample.shape
    ss_pad = 16
    depth = w_in.shape[0]
    n_pool = cache_k.shape[1]

    w_main, w_lora = jax.vmap(_permute_w_in)(w_in)
    w_branch_bf = w_branch.astype(BF16)
    w_out_bf = w_out.astype(BF16)
    cache_k2 = cache_k.reshape(depth, n_pool, PAGE_SIZE * 2 * A_HEADS, A_HD)
    cache_v2 = cache_v.reshape(depth, n_pool, PAGE_SIZE * A_HEADS, LANES)

    inv_a = ROPE_THETA ** (-jnp.arange(A_HD // 2, dtype=F32) / (A_HD // 2))
    inv_c = ROPE_THETA ** (-jnp.linspace(0.0, 1.0, C_KD // 2, dtype=F32))
    pos_p = jnp.arange(sp)
    pos_s = PAST_LEN + jnp.arange(ss_pad)
    tabs_a_p, tabs_c_p = _rope_tables(pos_p, inv_a), _rope_tables(pos_p, inv_c)
    tabs_a_s = tuple(jnp.tile(t, (bs, 1)) for t in _rope_tables(pos_s, inv_a))
    tabs_c_s = _rope_tables(pos_s, inv_c)

    xp = x_prompt.reshape(bp * sp, D_MODEL)
    xs = jnp.pad(x_sample, ((0, 0), (0, ss_pad - ss), (0, 0))).reshape(bs * ss_pad, D_MODEL)
    zero_shift = (jnp.zeros((bp, 1, BRANCH_W), F32),) * 3 + (jnp.zeros((bp, 1, LANES), F32),)
    rwkv0_p = jnp.zeros((bp, B_HEADS // 2, LANES, LANES), F32)
    ret0_p = jnp.zeros((bp, C_HEADS // 2, LANES, 2 * LANES), F32)

    outs = [[] for _ in range(10)]
    for l in range(depth):
        lp = _layer_params(l, w_main, w_lora, w_branch_bf, w_out_bf, norm_w, a_qnorm, a_knorm, a_lambda, a_subln, b_mu,
                           b_w0, b_w2, b_a0, b_a2, b_kk, b_ka, b_rk, b_gn_w, b_gn_b)

        def attend_p(q_bf, k_f32, k_bf, p, lp=lp):
            return _flash_call(q_bf, k_bf, p, lp["a_lambda"], lp["subln"], bp, sp, min(sp, 512), lp["lam_init"])

        xp, k_new, p, lora, rw, rt = _trunk_layer(xp, bp, sp, sp, lp, tabs_a_p, tabs_c_p, attend_p, zero_shift,
                                                  rwkv0_p, ret0_p, 64, 128)
        last = p.reshape(bp, sp, N_MAIN)[:, sp - 1]
        outs[0].append(k_new.reshape(bp, sp, 2 * A_HEADS, A_HD))
        outs[1].append(p.reshape(bp, sp, N_MAIN)[:, :, COL_AV:COL_AV + BRANCH_W].reshape(bp, sp, A_HEADS, LANES))
        outs[2].append(_blockdiag_to_pairs(rw))
        outs[3].append(_join_shift(last[:, COL_BR:COL_BR + 1024], last[:, COL_BK:COL_BK + 1024],
                                   last[:, COL_BV:COL_BV + 1024], lora.reshape(bp, sp, LANES)[:, sp - 1]))
        outs[4].append(_blockdiag_to_pairs(rt))

        def attend_s(q_bf, k_f32, k_bf, p, lp=lp, l=l):
            q4 = q_bf.reshape(bs, ss_pad, 2 * A_HEADS, A_HD).transpose(0, 2, 1, 3)
            kn = k_f32.reshape(bs, ss_pad, 2 * A_HEADS, A_HD)[:, :ss]
            kn = jnp.pad(kn, ((0, 0), (0, PAGE_SIZE - ss), (0, 0), (0, 0))).reshape(bs, PAGE_SIZE * 2 * A_HEADS, A_HD)
            p3 = p.reshape(bs, ss_pad, N_MAIN)[:, :ss]
            vn = p3[:, :, COL_AV:COL_AV + BRANCH_W].reshape(bs, ss, A_HEADS, LANES)
            vn = jnp.pad(vn, ((0, 0), (0, PAGE_SIZE - ss), (0, 0), (0, 0))).reshape(bs, PAGE_SIZE * A_HEADS, LANES)
            ga = p.reshape(bs, ss_pad, N_MAIN)[:, :, COL_AG:COL_AG + BRANCH_W]
            o = _paged_call(page_table, q4, cache_k2, cache_v2, kn, vn, ga, lp["a_lambda"], lp["subln"], l,
                            lp["lam_init"])
            return o.reshape(bs * ss_pad, BRANCH_W)

        sh = tuple(t[:, None, :] for t in _split_shift(state_shift[l]))
        rwkv0_s = _pairs_to_blockdiag(state_rwkv[l])
        ret0_s = _pairs_to_blockdiag(state_ret[l])
        xs, k_new, p, lora, rw, rt = _trunk_layer(xs, bs, ss_pad, ss, lp, tabs_a_s, tabs_c_s, attend_s, sh,
                                                  rwkv0_s, ret0_s, ss_pad, ss_pad)
        p3 = p.reshape(bs, ss_pad, N_MAIN)
        last = p3[:, ss - 1]
        outs[5].append(k_new.reshape(bs, ss_pad, 2 * A_HEADS, A_HD)[:, :ss])
        outs[6].append(p3[:, :ss, COL_AV:COL_AV + BRANCH_W].reshape(bs, ss, A_HEADS, LANES))
        outs[7].append(_blockdiag_to_pairs(rw))
        outs[8].append(_join_shift(last[:, COL_BR:COL_BR + 1024], last[:, COL_BK:COL_BK + 1024],
                                   last[:, COL_BV:COL_BV + 1024], lora.reshape(bs, ss_pad, LANES)[:, ss - 1]))
        outs[9].append(_blockdiag_to_pairs(rt))

    st = [jnp.stack(o) for o in outs]
    y_prompt = xp.reshape(bp, sp, D_MODEL)
    y_sample = xs.reshape(bs, ss_pad, D_MODEL)[:, :ss]
    return (y_prompt, y_sample, st[0], st[1], st[2], st[3], st[4], st[5], st[6], st[7], st[8], st[9])
```

```python
import functools
import math

import jax
import jax.numpy as jnp
from jax import lax
from jax.experimental import pallas as pl
from jax.experimental.pallas import tpu as pltpu

F32 = jnp.float32
BF16 = jnp.bfloat16

D_MODEL = 2048
DEPTH = 4
PAST_LEN = 16384
PAGE_SIZE = 128
BRANCH_W = 1024
A_HEADS = 8
A_HD = 64
B_HEADS = 16
B_LORA = 64
B_SHIFT = 3 * BRANCH_W + 2 * B_LORA
B_GN_EPS = 64e-5
C_HEADS = 8
C_KD = 64
ROPE_THETA = 10000.0
NORM_EPS = 1e-6
LANES = 128
HALF = 64
NEG_BIG = -1e30

COL_AQ, COL_AK, COL_AV, COL_AG = 0, 1024, 2048, 3072
COL_BR, COL_BK, COL_BV, COL_BG = 4096, 5120, 6144, 7168
COL_CQ, COL_CK, COL_CV, COL_CG = 8192, 8704, 9216, 10240
COL_GATE = 11264
N_MAIN = 17408
VMEM_LIMIT = 56 * 1024 * 1024


def _cparams(sem):
    return pltpu.CompilerParams(dimension_semantics=sem, vmem_limit_bytes=VMEM_LIMIT)


def _nt(a, b):
    return lax.dot_general(a, b, (((1,), (1,)), ((), ())), preferred_element_type=F32)


def _nn(a, b):
    return jnp.dot(a, b, preferred_element_type=F32)


def _silu(x):
    return x * (1.0 / (1.0 + jnp.exp(-x)))


def _sigmoid(x):
    return 1.0 / (1.0 + jnp.exp(-x))


def _first_head(shape):
    return lax.broadcasted_iota(jnp.int32, shape, len(shape) - 1) < HALF


def _seg_sum(x):
    first = _first_head(x.shape)
    zero = jnp.zeros_like(x)
    sa = jnp.sum(jnp.where(first, x, zero), axis=-1, keepdims=True)
    sb = jnp.sum(jnp.where(first, zero, x), axis=-1, keepdims=True)
    return jnp.where(first, sa, sb)


def _rope(x, cos, sin_signed):
    lane = lax.broadcasted_iota(jnp.int32, x.shape, 1)
    first_half = (lane % HALF) < (HALF // 2)
    partner = jnp.where(first_half, pltpu.roll(x, LANES - HALF // 2, 1), pltpu.roll(x, HALF // 2, 1))
    return x * cos + partner * sin_signed


def _rms_kernel(x_ref, w_ref, o_ref):
    x = x_ref[...]
    y = x * lax.rsqrt(jnp.mean(x * x, axis=-1, keepdims=True) + NORM_EPS)
    o_ref[...] = (y * w_ref[...]).astype(BF16)


def _rms_call(x2d, w, tm):
    m = x2d.shape[0]
    return pl.pallas_call(
        _rms_kernel,
        grid=(m // tm,),
        in_specs=[pl.BlockSpec((tm, D_MODEL), lambda i: (i, 0)), pl.BlockSpec((1, D_MODEL), lambda i: (0, 0))],
        out_specs=pl.BlockSpec((tm, D_MODEL), lambda i: (i, 0)),
        out_shape=jax.ShapeDtypeStruct((m, D_MODEL), BF16),
        compiler_params=_cparams(("parallel",)),
        name="rmsnorm",
    )(x2d, w)


def _mm_kernel(a_ref, b_ref, o_ref):
    o_ref[...] = _nn(a_ref[...], b_ref[...])


def _mm_call(a, b, layer, tm, tn, name):
    m, k = a.shape
    n = b.shape[2]
    return pl.pallas_call(
        _mm_kernel,
        grid=(m // tm, n // tn),
        in_specs=[pl.BlockSpec((tm, k), lambda i, j: (i, 0)), pl.BlockSpec((None, k, tn), lambda i, j: (layer, 0, j))],
        out_specs=pl.BlockSpec((tm, tn), lambda i, j: (i, j)),
        out_shape=jax.ShapeDtypeStruct((m, n), F32),
        compiler_params=_cparams(("parallel", "arbitrary")),
        name=name,
    )(a, b)


def _qk_kernel(p_ref, nw_ref, cos_ref, sin_ref, *o_refs, scale):
    x = p_ref[...]
    ms = _seg_sum(x * x) * (1.0 / HALF)
    x = x * lax.rsqrt(ms + NORM_EPS) * nw_ref[...]
    y = _rope(x, cos_ref[...], sin_ref[...])
    if scale != 1.0:
        y = y * scale
    for o_ref in o_refs:
        o_ref[...] = y.astype(o_ref.dtype)


def _qk_call(p, col0, nw, cos, sin, tm, scale, out_dtypes, name):
    m = p.shape[0]
    ntab = cos.shape[0] // tm
    cb = col0 // LANES
    nblk = BRANCH_W // LANES
    outs = pl.pallas_call(
        functools.partial(_qk_kernel, scale=scale),
        grid=(m // tm, nblk),
        in_specs=[pl.BlockSpec((tm, LANES), lambda i, c: (i, cb + c)),
                  pl.BlockSpec((1, LANES), lambda i, c: (0, 0)),
                  pl.BlockSpec((tm, LANES), lambda i, c: (i % ntab, 0)),
                  pl.BlockSpec((tm, LANES), lambda i, c: (i % ntab, 0))],
        out_specs=[pl.BlockSpec((tm, LANES), lambda i, c: (i, c)) for _ in out_dtypes],
        out_shape=[jax.ShapeDtypeStruct((m, BRANCH_W), dt) for dt in out_dtypes],
        compiler_params=_cparams(("parallel", "parallel")),
        name=name,
    )(p, nw, cos, sin)
    return outs


def _lambda_value(lam_ref, lam_init):
    lv = lam_ref[...]
    s1 = jnp.sum(lv[0:1] * lv[1:2], axis=-1, keepdims=True)
    s2 = jnp.sum(lv[2:3] * lv[3:4], axis=-1, keepdims=True)
    return jnp.exp(s1) - jnp.exp(s2) + lam_init


def _subln_gate(o1, o2, lam, sub, ga, lam_init):
    a = o1 - lam * o2
    y = a * lax.rsqrt(jnp.mean(a * a, axis=-1, keepdims=True) + NORM_EPS) * sub
    return (y * (1.0 - lam_init)) * _silu(ga)


def _flash_kernel(q_ref, k_ref, v_ref, ga_ref, lam_ref, sub_ref, o_ref, q2_ref, m_ref, l_ref, acc_ref, *, tq, lam_init):
    qi = pl.program_id(2)
    ki = pl.program_id(3)

    @pl.when(ki == 0)
    def _():
        q = q_ref[...]
        first = _first_head(q.shape)
        zero = jnp.zeros_like(q)
        q2_ref[0:tq, :] = jnp.where(first, q, zero)
        q2_ref[tq:2 * tq, :] = jnp.where(first, zero, q)
        m_ref[...] = jnp.full(m_ref.shape, NEG_BIG, F32)
        l_ref[...] = jnp.zeros(l_ref.shape, F32)
        acc_ref[...] = jnp.zeros(acc_ref.shape, F32)

    def accumulate(masked):
        s = _nt(q2_ref[...], k_ref[...])
        if masked:
            row = lax.broadcasted_iota(jnp.int32, s.shape, 0)
            col = lax.broadcasted_iota(jnp.int32, s.shape, 1)
            s = jnp.where(col <= jnp.where(row >= tq, row - tq, row), s, NEG_BIG)
        m_prev = m_ref[...]
        m_new = jnp.maximum(m_prev, jnp.max(s, axis=-1, keepdims=True))
        alpha = jnp.exp(m_prev - m_new)
        p = jnp.exp(s - m_new)
        l_ref[...] = alpha * l_ref[...] + jnp.sum(p, axis=-1, keepdims=True)
        acc_ref[...] = alpha * acc_ref[...] + _nn(p.astype(BF16), v_ref[...].astype(BF16))
        m_ref[...] = m_new

    @pl.when(ki < qi)
    def _():
        accumulate(False)

    @pl.when(ki == qi)
    def _():
        accumulate(True)
        o = acc_ref[...] / l_ref[...]
        lam = _lambda_value(lam_ref, lam_init)
        y = _subln_gate(o[0:tq], o[tq:2 * tq], lam, sub_ref[...], ga_ref[...], lam_init)
        o_ref[...] = y.astype(BF16)


def _flash_call(q, k, p, a_lambda, subln, bn, s, tq, lam_init):
    nq = s // tq
    vb, gb = COL_AV // LANES, COL_AG // LANES
    return pl.pallas_call(
        functools.partial(_flash_kernel, tq=tq, lam_init=lam_init),
        grid=(bn, A_HEADS, nq, nq),
        in_specs=[pl.BlockSpec((tq, LANES), lambda b, h, i, j: (b * nq + i, h)),
                  pl.BlockSpec((tq, LANES), lambda b, h, i, j: (b * nq + jnp.minimum(i, j), h)),
                  pl.BlockSpec((tq, LANES), lambda b, h, i, j: (b * nq + jnp.minimum(i, j), vb + h)),
                  pl.BlockSpec((tq, LANES), lambda b, h, i, j: (b * nq + i, gb + h)),
                  pl.BlockSpec((4, A_HD), lambda b, h, i, j: (0, 0)),
                  pl.BlockSpec((1, LANES), lambda b, h, i, j: (0, 0))],
        out_specs=pl.BlockSpec((tq, LANES), lambda b, h, i, j: (b * nq + i, h)),
        out_shape=jax.ShapeDtypeStruct((bn * s, BRANCH_W), BF16),
        scratch_shapes=[pltpu.VMEM((2 * tq, LANES), BF16), pltpu.VMEM((2 * tq, 1), F32),
                        pltpu.VMEM((2 * tq, 1), F32), pltpu.VMEM((2 * tq, LANES), F32)],
        compiler_params=_cparams(("parallel", "parallel", "parallel", "arbitrary")),
        name="diff_attn_prompt",
    )(q, k, p, p, a_lambda, subln)


PAGES_PER_STEP = 4


def _paged_kernel(pt_ref, q_ref, *refs, n_steps, group, n_new, lam_init):
    del pt_ref
    k_refs, v_refs = refs[0:group], refs[group:2 * group]
    kn_ref, vn_ref, ga_ref, lam_ref, sub_ref, o_ref, s_ref, m_ref, l_ref, acc_ref = refs[2 * group:]
    step = pl.program_id(1)
    rows_h = 2 * n_new

    @pl.when(step == 0)
    def _():
        m_ref[...] = jnp.full(m_ref.shape, NEG_BIG, F32)
        l_ref[...] = jnp.zeros(l_ref.shape, F32)
        acc_ref[...] = jnp.zeros(acc_ref.shape, F32)

    def block(k_list, v_list, is_new):
        width = PAGE_SIZE * len(k_list)
        for j, k_ref in enumerate(k_list):
            for h in range(A_HEADS):
                kh = k_ref[2 * h:2 * h + 2].reshape(2 * A_HD, PAGE_SIZE).astype(BF16)
                s_ref[rows_h * h:rows_h * (h + 1), PAGE_SIZE * j:PAGE_SIZE * (j + 1)] = _nn(q_ref[h], kh)
        s = s_ref[:, 0:width]
        if is_new:
            row = lax.broadcasted_iota(jnp.int32, s.shape, 0)
            col = lax.broadcasted_iota(jnp.int32, s.shape, 1)
            s = jnp.where(col <= (row & (n_new - 1)), s, NEG_BIG)
        m_prev = m_ref[...]
        m_new = jnp.maximum(m_prev, jnp.max(s, axis=-1, keepdims=True))
        alpha = jnp.exp(m_prev - m_new)
        p = jnp.exp(s - m_new)
        l_ref[...] = alpha * l_ref[...] + jnp.sum(p, axis=-1, keepdims=True)
        m_ref[...] = m_new
        pb = p.astype(BF16)
        for h in range(A_HEADS):
            rows = slice(rows_h * h, rows_h * (h + 1))
            pv = None
            for j, v_ref in enumerate(v_list):
                vh = v_ref[pl.ds(h, PAGE_SIZE, stride=A_HEADS), :].astype(BF16)
                d = _nn(pb[rows, PAGE_SIZE * j:PAGE_SIZE * (j + 1)], vh)
                pv = d if pv is None else pv + d
            acc_ref[h] = alpha[rows] * acc_ref[h] + pv

    @pl.when(step < n_steps)
    def _():
        block(k_refs, v_refs, False)

    @pl.when(step == n_steps)
    def _():
        block([kn_ref], [vn_ref], True)
        lam = _lambda_value(lam_ref, lam_init)
        linv = 1.0 / l_ref[...]
        for h in range(A_HEADS):
            o = acc_ref[h] * linv[rows_h * h:rows_h * (h + 1)]
            cols = slice(LANES * h, LANES * (h + 1))
            o_ref[:, cols] = _subln_gate(o[0:n_new], o[n_new:rows_h], lam, sub_ref[...], ga_ref[:, cols], lam_init)


def _paged_call(page_table, q_bd, cache_kt, cache_v2, knew, vnew, ga, a_lambda, subln, layer, lam_init):
    bn, n_pages = page_table.shape
    nsub = 2 * A_HEADS
    n_new = q_bd.shape[2] // 2
    group = math.gcd(PAGES_PER_STEP, n_pages)
    n_steps = n_pages // group
    last = n_pages - 1
    page = lambda j: (lambda b, p, pt: (layer, pt[b, jnp.minimum(p * group + j, last)], 0, 0, 0))
    page4 = lambda j: (lambda b, p, pt: (layer, pt[b, jnp.minimum(p * group + j, last)], 0, 0))
    k_specs = [pl.BlockSpec((None, None, nsub, A_HD, PAGE_SIZE), page(j)) for j in range(group)]
    v_specs = [pl.BlockSpec((None, None, PAGE_SIZE * A_HEADS, LANES), page4(j)) for j in range(group)]
    grid_spec = pltpu.PrefetchScalarGridSpec(
        num_scalar_prefetch=1,
        grid=(bn, n_steps + 1),
        in_specs=[pl.BlockSpec((None, A_HEADS, 2 * n_new, LANES), lambda b, p, pt: (b, 0, 0, 0))] + k_specs + v_specs + [
            pl.BlockSpec((None, nsub, A_HD, PAGE_SIZE), lambda b, p, pt: (b, 0, 0, 0)),
            pl.BlockSpec((None, PAGE_SIZE * A_HEADS, LANES), lambda b, p, pt: (b, 0, 0)),
            pl.BlockSpec((None, n_new, BRANCH_W), lambda b, p, pt: (b, 0, 0)),
            pl.BlockSpec((4, A_HD), lambda b, p, pt: (0, 0)),
            pl.BlockSpec((1, LANES), lambda b, p, pt: (0, 0))],
        out_specs=pl.BlockSpec((None, n_new, BRANCH_W), lambda b, p, pt: (b, 0, 0)),
        scratch_shapes=[pltpu.VMEM((nsub * n_new, PAGE_SIZE * group), F32), pltpu.VMEM((nsub * n_new, 1), F32),
                        pltpu.VMEM((nsub * n_new, 1), F32), pltpu.VMEM((A_HEADS, 2 * n_new, LANES), F32)],
    )
    return pl.pallas_call(
        functools.partial(_paged_kernel, n_steps=n_steps, group=group, n_new=n_new, lam_init=lam_init),
        grid_spec=grid_spec,
        out_shape=jax.ShapeDtypeStruct((bn, n_new, BRANCH_W), F32),
        compiler_params=_cparams(("parallel", "arbitrary")),
        name="diff_attn_sample",
    )(page_table, q_bd, *([cache_kt] * group), *([cache_v2] * group), knew, vnew, ga, a_lambda, subln)


def _shift_rows(x, prev_row):
    row = lax.broadcasted_iota(jnp.int32, x.shape, 0)
    return jnp.where(row == 0, prev_row, pltpu.roll(x, 1, 0))


def _cumsum_rows(x):
    n = x.shape[0]
    row = lax.broadcasted_iota(jnp.int32, x.shape, 0)
    d = 1
    while d < n:
        x = x + jnp.where(row >= d, pltpu.roll(x, d, 0), 0.0)
        d *= 2
    return x


def _stack2(z):
    first = _first_head(z.shape)
    zero = jnp.zeros_like(z)
    return jnp.concatenate([jnp.where(first, z, zero), jnp.where(first, zero, z)], axis=0)


def _rwkv_kernel(r_ref, k_ref, v_ref, g_ref, lo_ref, sr_ref, sk_ref, sv_ref, sl_ref, st0_ref,
                 mur_ref, muk_ref, muv_ref, mul_ref, w0_ref, a0_ref, kkw_ref, kaw_ref, rk_ref, gnw_ref, gnb_ref,
                 w2_ref, a2_ref, o_ref, st_ref, pr_ref, pk_ref, pv_ref, pl_ref, s_ref,
                 la_ref, rb_ref, v2_ref, bk_ref, x_ref, t_ref, mak_ref, mr_ref, gs_ref, u_ref, *, ln, n_valid):
    c = pl.program_id(1)
    npair = B_HEADS // 2
    l2, l4 = 2 * ln, 4 * ln

    @pl.when(c == 0)
    def _():
        pr_ref[...] = sr_ref[...]
        pk_ref[...] = sk_ref[...]
        pv_ref[...] = sv_ref[...]
        pl_ref[...] = sl_ref[...]
        s_ref[...] = st0_ref[...]

    def mix(x_ref_, prev_ref, mu_ref):
        x = x_ref_[...]
        xm = x + (_shift_rows(x, prev_ref[...]) - x) * mu_ref[...]
        prev_ref[...] = x[n_valid - 1:n_valid]
        return xm

    r = mix(r_ref, pr_ref, mur_ref)
    k = mix(k_ref, pk_ref, muk_ref)
    v = mix(v_ref, pv_ref, muv_ref)
    lo = mix(lo_ref, pl_ref, mul_ref)
    w_pre = w0_ref[...] + _nn(jnp.tanh(lo).astype(BF16), w2_ref[...])
    neg = -w_pre
    softplus = jnp.maximum(neg, 0.0) + jnp.log(1.0 + jnp.exp(-jnp.abs(neg)))
    logw = -jnp.exp(-softplus - 0.5)
    a = _sigmoid(a0_ref[...] + _nn(lo.astype(BF16), a2_ref[...]))
    kk = k * kkw_ref[...]
    k2 = k * (1.0 + (a - 1.0) * kaw_ref[...])
    if n_valid < ln:
        live = lax.broadcasted_iota(jnp.int32, logw.shape, 0) < n_valid
        logw = jnp.where(live, logw, 0.0)
        kk = jnp.where(live, kk, 0.0)
        k2 = jnp.where(live, k2, 0.0)
    cum = _cumsum_rows(logw)
    e_in = jnp.exp(cum)
    e_ex = jnp.exp(cum - logw)
    e_inv = jnp.exp(-cum)
    c_last = cum[ln - 1:ln]
    e_last = jnp.exp(c_last - cum)
    decay_l = jnp.exp(c_last)

    for pr in range(npair):
        cols = slice(LANES * pr, LANES * (pr + 1))
        kkp = kk[:, cols]
        kkp = kkp * lax.rsqrt(_seg_sum(kkp * kkp) + 1e-12)
        b = kkp * a[:, cols]
        k2p = k2[:, cols]
        la_ref[pr] = jnp.concatenate([_stack2(-kkp * e_ex[:, cols]), _stack2(r[:, cols] * e_in[:, cols])], 0).astype(BF16)
        rb_ref[pr] = jnp.concatenate([_stack2(b * e_inv[:, cols]), _stack2(k2p * e_inv[:, cols])], 0).astype(BF16)
        bk_ref[pr] = jnp.concatenate([_stack2(b * e_last[:, cols]), _stack2(k2p * e_last[:, cols])], 0).astype(BF16)
        v2_ref[pr] = _stack2(v[:, cols]).astype(BF16)

    rr = lax.broadcasted_iota(jnp.int32, (l2, l4), 0) & (ln - 1)
    cc_full = lax.broadcasted_iota(jnp.int32, (l2, l4), 1)
    cc = cc_full & (ln - 1)
    strict_right = (cc < rr) & (cc_full >= l2)
    incl = cc <= rr
    rs = lax.broadcasted_iota(jnp.int32, (l2, l2), 0)
    cs = lax.broadcasted_iota(jnp.int32, (l2, l2), 1)
    strict_sq = (cs & (ln - 1)) < (rs & (ln - 1))
    eye = (rs == cs).astype(F32)
    for pr in range(npair):
        la = la_ref[pr]
        mm = _nt(la, rb_ref[pr])
        xa = mm[0:l2]
        x = jnp.where(strict_sq, xa[:, 0:l2], 0.0)
        x_ref[pr] = x
        t_ref[pr] = eye + x
        mak_ref[pr] = jnp.where(strict_right, xa, 0.0).astype(BF16)
        mr_ref[pr] = jnp.where(incl, mm[l2:l4], 0.0).astype(BF16)
        gs_ref[pr] = _nt(la, s_ref[pr].astype(BF16))
    for pr in range(npair):
        v2 = v2_ref[pr]
        u_ref[pr] = gs_ref[pr, 0:l2, :] + _nn(mak_ref[pr], jnp.concatenate([v2, v2], axis=0))

    n = 1
    while 2 * n < ln:
        for pr in range(npair):
            xb = x_ref[pr].astype(BF16)
            x_ref[pr] = _nn(xb, xb)
        for pr in range(npair):
            t = t_ref[pr]
            t_ref[pr] = t + _nn(t.astype(BF16), x_ref[pr].astype(BF16))
        n *= 2

    for pr in range(npair):
        u_ref[pr] = _nn(t_ref[pr].astype(BF16), u_ref[pr].astype(BF16))

    rk_all = r * k2 * rk_ref[...]
    gate = _silu(g_ref[...])
    for pr in range(npair):
        cols = slice(LANES * pr, LANES * (pr + 1))
        u2 = u_ref[pr]
        v2 = v2_ref[pr]
        y2 = gs_ref[pr, l2:l4, :] + _nn(mr_ref[pr], jnp.concatenate([u2.astype(BF16), v2], axis=0))
        y = y2[0:ln] + y2[ln:l2]
        uvt = jnp.concatenate([u2, v2.astype(F32)], axis=0).T.astype(BF16)
        s_ref[pr] = s_ref[pr] * decay_l[:, cols] + _nn(uvt, bk_ref[pr])
        mu = _seg_sum(y) * (1.0 / HALF)
        dy = y - mu
        var = _seg_sum(dy * dy) * (1.0 / HALF)
        yn = dy * lax.rsqrt(var + B_GN_EPS) * gnw_ref[:, cols] + gnb_ref[:, cols]
        yn = yn + _seg_sum(rk_all[:, cols]) * v[:, cols]
        o_ref[:, cols] = (yn * gate[:, cols]).astype(BF16)

    @pl.when(c == pl.num_programs(1) - 1)
    def _():
        st_ref[...] = s_ref[...]


def _rwkv_call(p, lora, shift_r, shift_k, shift_v, shift_l, state_bd, prm, bn, s, ln, n_valid):
    nc = s // ln
    npair = B_HEADS // 2
    rb, kb, vb, gb = (c // BRANCH_W for c in (COL_BR, COL_BK, COL_BV, COL_BG))
    tok = lambda cb: pl.BlockSpec((ln, BRANCH_W), lambda b, c: (b * nc + c, cb))
    per_b = lambda w: pl.BlockSpec((None, 1, w), lambda b, c: (b, 0, 0))
    row = lambda w: pl.BlockSpec((1, w), lambda b, c: (0, 0))
    st_spec = pl.BlockSpec((None, npair, LANES, LANES), lambda b, c: (b, 0, 0, 0))
    l2, l4 = 2 * ln, 4 * ln
    return pl.pallas_call(
        functools.partial(_rwkv_kernel, ln=ln, n_valid=n_valid),
        grid=(bn, nc),
        in_specs=[tok(rb), tok(kb), tok(vb), tok(gb),
                  pl.BlockSpec((ln, LANES), lambda b, c: (b * nc + c, 0)),
                  per_b(BRANCH_W), per_b(BRANCH_W), per_b(BRANCH_W), per_b(LANES), st_spec,
                  row(BRANCH_W), row(BRANCH_W), row(BRANCH_W), row(LANES),
                  row(BRANCH_W), row(BRANCH_W), row(BRANCH_W), row(BRANCH_W), row(BRANCH_W), row(BRANCH_W),
                  row(BRANCH_W),
                  pl.BlockSpec((LANES, BRANCH_W), lambda b, c: (0, 0)),
                  pl.BlockSpec((LANES, BRANCH_W), lambda b, c: (0, 0))],
        out_specs=[pl.BlockSpec((ln, BRANCH_W), lambda b, c: (b * nc + c, 0)), st_spec],
        out_shape=[jax.ShapeDtypeStruct((bn * s, BRANCH_W), BF16),
                   jax.ShapeDtypeStruct((bn, npair, LANES, LANES), F32)],
        scratch_shapes=[pltpu.VMEM((1, BRANCH_W), F32), pltpu.VMEM((1, BRANCH_W), F32),
                        pltpu.VMEM((1, BRANCH_W), F32), pltpu.VMEM((1, LANES), F32),
                        pltpu.VMEM((npair, LANES, LANES), F32),
                        pltpu.VMEM((npair, l4, LANES), BF16), pltpu.VMEM((npair, l4, LANES), BF16),
                        pltpu.VMEM((npair, l2, LANES), BF16), pltpu.VMEM((npair, l4, LANES), BF16),
                        pltpu.VMEM((npair, l2, l2), F32), pltpu.VMEM((npair, l2, l2), F32),
                        pltpu.VMEM((npair, l2, l4), BF16), pltpu.VMEM((npair, l2, l4), BF16),
                        pltpu.VMEM((npair, l4, LANES), F32), pltpu.VMEM((npair, l2, LANES), F32)],
        compiler_params=_cparams(("parallel", "arbitrary")),
        name="rwkv7_chunked",
    )(p, p, p, p, lora, shift_r, shift_k, shift_v, shift_l, state_bd,
      prm["mu_r"], prm["mu_k"], prm["mu_v"], prm["mu_l"], prm["w0"], prm["a0"], prm["kk"], prm["ka"], prm["rk"],
      prm["gn_w"], prm["gn_b"], prm["w2"], prm["a2"])


def _ret_kernel(q_ref, k_ref, v_ref, g_ref, cos_ref, sin_ref, st0_ref, o_ref, st_ref, s_ref, *, cn, n_valid):
    c = pl.program_id(1)

    @pl.when(c == 0)
    def _():
        s_ref[...] = st0_ref[...]

    cos = cos_ref[...]
    sin = sin_ref[...]
    rowc = lax.broadcasted_iota(jnp.int32, (cn, cn), 0)
    colc = lax.broadcasted_iota(jnp.int32, (cn, cn), 1)
    dist = (rowc - colc).astype(F32)
    causal = colc <= rowc
    n_idx = lax.broadcasted_iota(jnp.int32, (cn, 1), 0).astype(F32)
    first128 = _first_head((1, LANES))
    first256 = lax.broadcasted_iota(jnp.int32, (1, 2 * LANES), 1) < LANES
    ri = lax.broadcasted_iota(jnp.int32, (LANES, 2 * LANES), 0) < HALF
    ci = lax.broadcasted_iota(jnp.int32, (LANES, 2 * LANES), 1) < LANES
    live = lax.broadcasted_iota(jnp.int32, (cn, 1), 0) < n_valid
    for pr in range(C_HEADS // 2):
        lg = [math.log(1.0 - 2.0 ** (-5.0 - (2 * pr + j))) for j in range(2)]
        qc = slice(LANES * pr, LANES * (pr + 1))
        vc = slice(2 * LANES * pr, 2 * LANES * (pr + 1))
        q = _rope(q_ref[:, qc], cos, sin)
        k = _rope(k_ref[:, qc], cos, sin) * (C_KD ** -0.5)
        if n_valid < cn:
            k = jnp.where(live, k, 0.0)
        v = v_ref[:, vc]
        vb = v.astype(BF16)
        kb = k.astype(BF16)
        zero = jnp.zeros_like(q)
        outs = []
        for j in range(2):
            qj = jnp.where(first128, q, zero) if j == 0 else jnp.where(first128, zero, q)
            sc = _nt(qj.astype(BF16), kb)
            sc = sc * jnp.where(causal, jnp.exp(jnp.maximum(dist, 0.0) * lg[j]), 0.0)
            outs.append(_nn(sc.astype(BF16), vb[:, LANES * j:LANES * (j + 1)]))
        st = s_ref[pr]
        xi = jnp.where(first256, jnp.exp((n_idx + 1.0) * lg[0]), jnp.exp((n_idx + 1.0) * lg[1]))
        o = jnp.concatenate(outs, axis=1) + _nn(q.astype(BF16), st.astype(BF16)) * xi
        zeta = jnp.where(first128, jnp.exp((n_valid - 1.0 - n_idx) * lg[0]), jnp.exp((n_valid - 1.0 - n_idx) * lg[1]))
        upd = _nn((k * zeta).T.astype(BF16), vb)
        gch = jnp.where(first256, math.exp(n_valid * lg[0]), math.exp(n_valid * lg[1]))
        s_ref[pr] = st * gch + jnp.where(ri == ci, upd, 0.0)
        gate = _silu(g_ref[:, vc])
        for j in range(2):
            oj = o[:, LANES * j:LANES * (j + 1)]
            oj = oj * lax.rsqrt(jnp.mean(oj * oj, axis=-1, keepdims=True) + NORM_EPS)
            o_ref[:, 2 * LANES * pr + LANES * j:2 * LANES * pr + LANES * (j + 1)] = (
                oj * gate[:, LANES * j:LANES * (j + 1)]).astype(BF16)

    @pl.when(c == pl.num_programs(1) - 1)
    def _():
        st_ref[...] = s_ref[...]


def _ret_call(p, cos, sin, state_bd, bn, s, cn, n_valid):
    nc = s // cn
    npair = C_HEADS // 2
    ntab = cos.shape[0] // cn
    st_spec = pl.BlockSpec((None, npair, LANES, 2 * LANES), lambda b, c: (b, 0, 0, 0))
    return pl.pallas_call(
        functools.partial(_ret_kernel, cn=cn, n_valid=n_valid),
        grid=(bn, nc),
        in_specs=[pl.BlockSpec((cn, 512), lambda b, c: (b * nc + c, COL_CQ // 512)),
                  pl.BlockSpec((cn, 512), lambda b, c: (b * nc + c, COL_CK // 512)),
                  pl.BlockSpec((cn, BRANCH_W), lambda b, c: (b * nc + c, COL_CV // BRANCH_W)),
                  pl.BlockSpec((cn, BRANCH_W), lambda b, c: (b * nc + c, COL_CG // BRANCH_W)),
                  pl.BlockSpec((cn, LANES), lambda b, c: (c % ntab, 0)),
                  pl.BlockSpec((cn, LANES), lambda b, c: (c % ntab, 0)),
                  st_spec],
        out_specs=[pl.BlockSpec((cn, BRANCH_W), lambda b, c: (b * nc + c, 0)), st_spec],
        out_shape=[jax.ShapeDtypeStruct((bn * s, BRANCH_W), BF16),
                   jax.ShapeDtypeStruct((bn, npair, LANES, 2 * LANES), F32)],
        scratch_shapes=[pltpu.VMEM((npair, LANES, 2 * LANES), F32)],
        compiler_params=_cparams(("parallel", "arbitrary")),
        name="retention_chunkwise",
    )(p, p, p, p, cos, sin, state_bd)


def _merge_kernel(oa_ref, ob_ref, oc_ref, wb_ref, g0_ref, g1_ref, g2_ref, o_ref):
    acc = _sigmoid(g0_ref[...]) * _nn(oa_ref[...], wb_ref[0])
    acc = acc + _sigmoid(g1_ref[...]) * _nn(ob_ref[...], wb_ref[1])
    acc = acc + _sigmoid(g2_ref[...]) * _nn(oc_ref[...], wb_ref[2])
    o_ref[...] = acc.astype(BF16)


def _merge_call(oa, ob, oc, wb, layer, p, tm, tn):
    m = oa.shape[0]
    nj = D_MODEL // tn
    act = pl.BlockSpec((tm, BRANCH_W), lambda i, j: (i, 0))
    gate = lambda n: pl.BlockSpec((tm, tn), lambda i, j: (i, (COL_GATE + n * D_MODEL) // tn + j))
    return pl.pallas_call(
        _merge_kernel,
        grid=(m // tm, nj),
        in_specs=[act, act, act, pl.BlockSpec((None, 3, BRANCH_W, tn), lambda i, j: (layer, 0, 0, j)), gate(0), gate(1), gate(2)],
        out_specs=pl.BlockSpec((tm, tn), lambda i, j: (i, j)),
        out_shape=jax.ShapeDtypeStruct((m, D_MODEL), BF16),
        compiler_params=_cparams(("parallel", "arbitrary")),
        name="branch_merge",
    )(oa, ob, oc, wb, p, p, p)


def _out_kernel(a_ref, b_ref, x_ref, o_ref):
    o_ref[...] = x_ref[...] + _nn(a_ref[...], b_ref[...])


def _out_call(merged, w_out, layer, x2d, tm, tn):
    m = merged.shape[0]
    return pl.pallas_call(
        _out_kernel,
        grid=(m // tm, D_MODEL // tn),
        in_specs=[pl.BlockSpec((tm, D_MODEL), lambda i, j: (i, 0)), pl.BlockSpec((None, D_MODEL, tn), lambda i, j: (layer, 0, j)),
                  pl.BlockSpec((tm, tn), lambda i, j: (i, j))],
        out_specs=pl.BlockSpec((tm, tn), lambda i, j: (i, j)),
        out_shape=jax.ShapeDtypeStruct((m, D_MODEL), F32),
        compiler_params=_cparams(("parallel", "arbitrary")),
        name="out_proj_residual",
    )(merged, w_out, x2d)


def _rope_tables(pos, inv_freq):
    ang = pos.astype(F32)[:, None] * inv_freq[None, :]
    cos = jnp.cos(ang)
    sin = jnp.sin(ang)
    cos = jnp.concatenate([cos, cos, cos, cos], axis=1)
    sin = jnp.concatenate([-sin, sin, -sin, sin], axis=1)
    return cos, sin


def _permute_w_in(w):
    b0 = 4096
    c0 = b0 + B_SHIFT + BRANCH_W
    cut = lambda lo, hi: w[..., lo:hi].astype(BF16)
    main = jnp.concatenate([cut(0, b0 + 1024), cut(b0 + 1088, b0 + 3136), cut(b0 + 3200, b0 + 4224),
                            cut(c0, c0 + 3072 + 3 * D_MODEL)], axis=-1)
    lora = jnp.concatenate([cut(b0 + 1024, b0 + 1088), cut(b0 + 3136, b0 + 3200)], axis=-1)
    return main, lora


def _split_shift(sh):
    return (sh[..., 0:1024], sh[..., 1088:2112], sh[..., 2112:3136],
            jnp.concatenate([sh[..., 1024:1088], sh[..., 3136:3200]], axis=-1))


def _join_shift(r, k, v, lo):
    return jnp.concatenate([r, lo[..., 0:64], k, v, lo[..., 64:128]], axis=-1)


def _pairs_to_blockdiag(st):
    bn, nh, r, c = st.shape
    st = st.reshape(bn, nh // 2, 2, r, c)
    z = jnp.zeros_like(st[:, :, 0])
    top = jnp.concatenate([st[:, :, 0], z], axis=-1)
    bot = jnp.concatenate([z, st[:, :, 1]], axis=-1)
    return jnp.concatenate([top, bot], axis=-2)


def _blockdiag_to_pairs(bd):
    bn, npair, r2, c2 = bd.shape
    r, c = r2 // 2, c2 // 2
    return jnp.stack([bd[:, :, :r, :c], bd[:, :, r:, c:]], axis=2).reshape(bn, 2 * npair, r, c)


def _layer_params(l, w_main, w_lora, w_branch_bf, w_out_bf, norm_w, a_qnorm, a_knorm, a_lambda, a_subln, b_mu, b_w0,
                  b_w2, b_a0, b_a2, b_kk, b_ka, b_rk, b_gn_w, b_gn_b):
    mu_r, mu_k, mu_v, mu_l = _split_shift(b_mu[l])
    zeros = jnp.zeros((B_LORA, BRANCH_W), F32)
    row = lambda t: t.reshape(1, -1)
    return dict(
        layer=l, w_main=w_main, w_lora=w_lora, w_branch=w_branch_bf, w_out=w_out_bf, norm_w=row(norm_w[l]),
        qn=row(jnp.tile(a_qnorm[l], 2)), kn=row(jnp.tile(a_knorm[l], 2)), a_lambda=a_lambda[l], subln=row(a_subln[l]),
        lam_init=0.8 - 0.6 * math.exp(-0.3 * l),
        rwkv=dict(mu_r=row(mu_r), mu_k=row(mu_k), mu_v=row(mu_v), mu_l=row(mu_l), w0=row(b_w0[l]), a0=row(b_a0[l]),
                  kk=row(b_kk[l]), ka=row(b_ka[l]), rk=row(b_rk[l]), gn_w=row(b_gn_w[l]), gn_b=row(b_gn_b[l]),
                  w2=jnp.concatenate([b_w2[l], zeros], axis=0).astype(BF16),
                  a2=jnp.concatenate([zeros, b_a2[l]], axis=0).astype(BF16)))


def _tiles(m):
    return min(m, 1024)


def _trunk_layer(x2d, bn, s, n_valid, lp, tabs_a, tabs_c, attend, shift0, rwkv0_bd, ret0_bd, chunk_b, chunk_c):
    m = bn * s
    tm = _tiles(m)
    h = _rms_call(x2d, lp["norm_w"], min(m, 512))
    p = _mm_call(h, lp["w_main"], lp["layer"], tm, 1024, "in_proj")
    lora = _mm_call(h, lp["w_lora"], lp["layer"], tm, LANES, "in_proj_lora")
    t_qk = min(tabs_a[0].shape[0], 512)
    (q_bf,) = _qk_call(p, COL_AQ, lp["qn"], tabs_a[0], tabs_a[1], t_qk, A_HD ** -0.5, (BF16,), "q_norm_rope")
    k_f32, k_bf = _qk_call(p, COL_AK, lp["kn"], tabs_a[0], tabs_a[1], t_qk, 1.0, (F32, BF16), "k_norm_rope")
    oa = attend(q_bf, k_f32, k_bf, p)
    sr, sk, sv, sl = shift0
    ob, rwkv_bd = _rwkv_call(p, lora, sr, sk, sv, sl, rwkv0_bd, lp["rwkv"], bn, s, chunk_b, min(n_valid, chunk_b))
    oc, ret_bd = _ret_call(p, tabs_c[0], tabs_c[1], ret0_bd, bn, s, chunk_c, min(n_valid, chunk_c))
    merged = _merge_call(oa, ob, oc, lp["w_branch"], lp["layer"], p, min(m, 512), 512)
    y = _out_call(merged, lp["w_out"], lp["layer"], x2d, tm, 1024)
    return y, k_f32, p, lora, rwkv_bd, ret_bd


def kernel(x_prompt, x_sample, cache_k, cache_v, page_table, state_rwkv, state_shift, state_ret, norm_w, w_in, a_qnorm, a_knorm, a_lambda, a_subln, b_mu, b_w0, b_w2, b_a0, b_a2, b_kk, b_ka, b_rk, b_gn_w, b_gn_b, w_branch, w_out):
    bp, sp, _ = x_prompt.shape
    bs, ss, _ = x_sample.shape
    ss_pad = 16
    depth = w_in.shape[0]
    n_pool = cache_k.shape[1]

    w_main, w_lora = _permute_w_in(w_in)
    w_branch_bf = w_branch.astype(BF16)
    w_out_bf = w_out.astype(BF16)
    cache_kt = jnp.transpose(cache_k, (0, 1, 3, 4, 2))
    cache_v2 = cache_v.reshape(depth, n_pool, PAGE_SIZE * A_HEADS, LANES)

    inv_a = ROPE_THETA ** (-jnp.arange(A_HD // 2, dtype=F32) / (A_HD // 2))
    inv_c = ROPE_THETA ** (-jnp.linspace(0.0, 1.0, C_KD // 2, dtype=F32))
    pos_p = jnp.arange(sp)
    pos_s = PAST_LEN + jnp.arange(ss_pad)
    tabs_a_p, tabs_c_p = _rope_tables(pos_p, inv_a), _rope_tables(pos_p, inv_c)
    tabs_a_s = tuple(jnp.tile(t, (bs, 1)) for t in _rope_tables(pos_s, inv_a))
    tabs_c_s = _rope_tables(pos_s, inv_c)

    xp = x_prompt.reshape(bp * sp, D_MODEL)
    xs = jnp.pad(x_sample, ((0, 0), (0, ss_pad - ss), (0, 0))).reshape(bs * ss_pad, D_MODEL)
    zero_shift = (jnp.zeros((bp, 1, BRANCH_W), F32),) * 3 + (jnp.zeros((bp, 1, LANES), F32),)
    rwkv0_p = jnp.zeros((bp, B_HEADS // 2, LANES, LANES), F32)
    ret0_p = jnp.zeros((bp, C_HEADS // 2, LANES, 2 * LANES), F32)

    outs = [[] for _ in range(10)]
    for l in range(depth):
        lp = _layer_params(l, w_main, w_lora, w_branch_bf, w_out_bf, norm_w, a_qnorm, a_knorm, a_lambda, a_subln, b_mu,
                           b_w0, b_w2, b_a0, b_a2, b_kk, b_ka, b_rk, b_gn_w, b_gn_b)

        def attend_p(q_bf, k_f32, k_bf, p, lp=lp):
            return _flash_call(q_bf, k_bf, p, lp["a_lambda"], lp["subln"], bp, sp, min(sp, 512), lp["lam_init"])

        xp, k_new, p, lora, rw, rt = _trunk_layer(xp, bp, sp, sp, lp, tabs_a_p, tabs_c_p, attend_p, zero_shift,
                                                  rwkv0_p, ret0_p, 64, 128)
        last = p.reshape(bp, sp, N_MAIN)[:, sp - 1]
        outs[0].append(k_new.reshape(bp, sp, 2 * A_HEADS, A_HD))
        outs[1].append(p.reshape(bp, sp, N_MAIN)[:, :, COL_AV:COL_AV + BRANCH_W].reshape(bp, sp, A_HEADS, LANES))
        outs[2].append(_blockdiag_to_pairs(rw))
        outs[3].append(_join_shift(last[:, COL_BR:COL_BR + 1024], last[:, COL_BK:COL_BK + 1024],
                                   last[:, COL_BV:COL_BV + 1024], lora.reshape(bp, sp, LANES)[:, sp - 1]))
        outs[4].append(_blockdiag_to_pairs(rt))

        def attend_s(q_bf, k_f32, k_bf, p, lp=lp, l=l):
            q5 = q_bf.reshape(bs, ss_pad, A_HEADS, 2, A_HD)[:, :ss].transpose(0, 2, 3, 1, 4)
            zq = jnp.zeros_like(q5[:, :, 0])
            q_bd = jnp.concatenate([jnp.concatenate([q5[:, :, 0], zq], axis=-1),
                                    jnp.concatenate([zq, q5[:, :, 1]], axis=-1)], axis=2)
            kn = k_f32.reshape(bs, ss_pad, 2 * A_HEADS, A_HD)[:, :ss].transpose(0, 2, 3, 1)
            kn = jnp.pad(kn, ((0, 0), (0, 0), (0, 0), (0, PAGE_SIZE - ss)))
            p3 = p.reshape(bs, ss_pad, N_MAIN)[:, :ss]
            vn = p3[:, :, COL_AV:COL_AV + BRANCH_W].reshape(bs, ss, A_HEADS, LANES)
            vn = jnp.pad(vn, ((0, 0), (0, PAGE_SIZE - ss), (0, 0), (0, 0))).reshape(bs, PAGE_SIZE * A_HEADS, LANES)
            ga = p3[:, :, COL_AG:COL_AG + BRANCH_W]
            o = _paged_call(page_table, q_bd, cache_kt, cache_v2, kn, vn, ga, lp["a_lambda"], lp["subln"], l,
                            lp["lam_init"])
            return jnp.pad(o, ((0, 0), (0, ss_pad - ss), (0, 0))).astype(BF16).reshape(bs * ss_pad, BRANCH_W)

        sh = tuple(t[:, None, :] for t in _split_shift(state_shift[l]))
        rwkv0_s = _pairs_to_blockdiag(state_rwkv[l])
        ret0_s = _pairs_to_blockdiag(state_ret[l])
        xs, k_new, p, lora, rw, rt = _trunk_layer(xs, bs, ss_pad, ss, lp, tabs_a_s, tabs_c_s, attend_s, sh,
                                                  rwkv0_s, ret0_s, ss_pad, ss_pad)
        p3 = p.reshape(bs, ss_pad, N_MAIN)
        last = p3[:, ss - 1]
        outs[5].append(k_new.reshape(bs, ss_pad, 2 * A_HEADS, A_HD)[:, :ss])
        outs[6].append(p3[:, :ss, COL_AV:COL_AV + BRANCH_W].reshape(bs, ss, A_HEADS, LANES))
        outs[7].append(_blockdiag_to_pairs(rw))
        outs[8].append(_join_shift(last[:, COL_BR:COL_BR + 1024], last[:, COL_BK:COL_BK + 1024],
                                   last[:, COL_BV:COL_BV + 1024], lora.reshape(bs, ss_pad, LANES)[:, ss - 1]))
        outs[9].append(_blockdiag_to_pairs(rt))

    st = [jnp.stack(o) for o in outs]
    y_prompt = xp.reshape(bp, sp, D_MODEL)
    y_sample = xs.reshape(bs, ss_pad, D_MODEL)[:, :ss]
    return (y_prompt, y_sample, st[0], st[1], st[2], st[3], st[4], st[5], st[6], st[7], st[8], st[9])
```

```python
import functools
import math

import jax
import jax.numpy as jnp
from jax import lax
from jax.experimental import pallas as pl
from jax.experimental.pallas import tpu as pltpu

F32 = jnp.float32
BF16 = jnp.bfloat16

D_MODEL = 2048
DEPTH = 4
PAST_LEN = 16384
PAGE_SIZE = 128
BRANCH_W = 1024
A_HEADS = 8
A_HD = 64
B_HEADS = 16
B_LORA = 64
B_SHIFT = 3 * BRANCH_W + 2 * B_LORA
B_GN_EPS = 64e-5
C_HEADS = 8
C_KD = 64
ROPE_THETA = 10000.0
NORM_EPS = 1e-6
LANES = 128
HALF = 64
NEG_BIG = -1e30

COL_AQ, COL_AK, COL_AV, COL_AG = 0, 1024, 2048, 3072
COL_BR, COL_BK, COL_BV, COL_BG = 4096, 5120, 6144, 7168
COL_CQ, COL_CK, COL_CV, COL_CG = 8192, 8704, 9216, 10240
COL_GATE = 11264
N_MAIN = 17408
VMEM_LIMIT = 56 * 1024 * 1024


def _cparams(sem):
    return pltpu.CompilerParams(dimension_semantics=sem, vmem_limit_bytes=VMEM_LIMIT)


def _nt(a, b):
    return lax.dot_general(a, b, (((1,), (1,)), ((), ())), preferred_element_type=F32)


def _nn(a, b):
    return jnp.dot(a, b, preferred_element_type=F32)


def _silu(x):
    return x * (1.0 / (1.0 + jnp.exp(-x)))


def _sigmoid(x):
    return 1.0 / (1.0 + jnp.exp(-x))


def _first_head(shape):
    return lax.broadcasted_iota(jnp.int32, shape, len(shape) - 1) < HALF


def _seg_sum(x):
    first = _first_head(x.shape)
    zero = jnp.zeros_like(x)
    sa = jnp.sum(jnp.where(first, x, zero), axis=-1, keepdims=True)
    sb = jnp.sum(jnp.where(first, zero, x), axis=-1, keepdims=True)
    return jnp.where(first, sa, sb)


def _rope(x, cos, sin_signed):
    lane = lax.broadcasted_iota(jnp.int32, x.shape, 1)
    first_half = (lane % HALF) < (HALF // 2)
    partner = jnp.where(first_half, pltpu.roll(x, LANES - HALF // 2, 1), pltpu.roll(x, HALF // 2, 1))
    return x * cos + partner * sin_signed


def _rms_kernel(x_ref, w_ref, o_ref):
    x = x_ref[...]
    y = x * lax.rsqrt(jnp.mean(x * x, axis=-1, keepdims=True) + NORM_EPS)
    o_ref[...] = (y * w_ref[...]).astype(BF16)


def _rms_call(x2d, w, tm):
    m = x2d.shape[0]
    return pl.pallas_call(
        _rms_kernel,
        grid=(m // tm,),
        in_specs=[pl.BlockSpec((tm, D_MODEL), lambda i: (i, 0)), pl.BlockSpec((1, D_MODEL), lambda i: (0, 0))],
        out_specs=pl.BlockSpec((tm, D_MODEL), lambda i: (i, 0)),
        out_shape=jax.ShapeDtypeStruct((m, D_MODEL), BF16),
        compiler_params=_cparams(("parallel",)),
        name="rmsnorm",
    )(x2d, w)


def _mm_kernel(a_ref, b_ref, o_ref):
    o_ref[...] = _nn(a_ref[...], b_ref[...])


def _mm_call(a, b, layer, tm, tn, name):
    m, k = a.shape
    n = b.shape[2]
    return pl.pallas_call(
        _mm_kernel,
        grid=(m // tm, n // tn),
        in_specs=[pl.BlockSpec((tm, k), lambda i, j: (i, 0)), pl.BlockSpec((None, k, tn), lambda i, j: (layer, 0, j))],
        out_specs=pl.BlockSpec((tm, tn), lambda i, j: (i, j)),
        out_shape=jax.ShapeDtypeStruct((m, n), F32),
        compiler_params=_cparams(("parallel", "arbitrary")),
        name=name,
    )(a, b)


def _qk_kernel(p_ref, qn_ref, kn_ref, cos_ref, sin_ref, q_ref, k32_ref, kbf_ref):
    cos = cos_ref[...]
    sin = sin_ref[...]

    def norm_rope(x, nw):
        ms = _seg_sum(x * x) * (1.0 / HALF)
        return _rope(x * lax.rsqrt(ms + NORM_EPS) * nw, cos, sin)

    for c in range(BRANCH_W // LANES):
        cols = slice(LANES * c, LANES * (c + 1))
        q_ref[:, cols] = (norm_rope(p_ref[:, cols], qn_ref[...]) * (A_HD ** -0.5)).astype(BF16)
        kc = norm_rope(p_ref[:, BRANCH_W + LANES * c:BRANCH_W + LANES * (c + 1)], kn_ref[...])
        k32_ref[:, cols] = kc
        kbf_ref[:, cols] = kc.astype(BF16)


def _qk_call(p, qn, kn, cos, sin, tm):
    m = p.shape[0]
    ntab = cos.shape[0] // tm
    blk = pl.BlockSpec((tm, BRANCH_W), lambda i: (i, 0))
    return pl.pallas_call(
        _qk_kernel,
        grid=(m // tm,),
        in_specs=[pl.BlockSpec((tm, 2 * BRANCH_W), lambda i: (i, 0)),
                  pl.BlockSpec((1, LANES), lambda i: (0, 0)), pl.BlockSpec((1, LANES), lambda i: (0, 0)),
                  pl.BlockSpec((tm, LANES), lambda i: (i % ntab, 0)), pl.BlockSpec((tm, LANES), lambda i: (i % ntab, 0))],
        out_specs=[blk, blk, blk],
        out_shape=[jax.ShapeDtypeStruct((m, BRANCH_W), BF16), jax.ShapeDtypeStruct((m, BRANCH_W), F32),
                   jax.ShapeDtypeStruct((m, BRANCH_W), BF16)],
        compiler_params=_cparams(("parallel",)),
        name="qk_norm_rope",
    )(p, qn, kn, cos, sin)


def _lambda_value(lam_ref, lam_init):
    lv = lam_ref[...]
    s1 = jnp.sum(lv[0:1] * lv[1:2], axis=-1, keepdims=True)
    s2 = jnp.sum(lv[2:3] * lv[3:4], axis=-1, keepdims=True)
    return jnp.exp(s1) - jnp.exp(s2) + lam_init


def _subln_gate(o1, o2, lam, sub, ga, lam_init):
    a = o1 - lam * o2
    y = a * lax.rsqrt(jnp.mean(a * a, axis=-1, keepdims=True) + NORM_EPS) * sub
    return (y * (1.0 - lam_init)) * _silu(ga)


def _flash_kernel(q_ref, k_ref, v_ref, ga_ref, lam_ref, sub_ref, o_ref, q2_ref, m_ref, l_ref, acc_ref, *, tq, lam_init):
    qi = pl.program_id(2)
    ki = pl.program_id(3)

    @pl.when(ki == 0)
    def _():
        q = q_ref[...]
        first = _first_head(q.shape)
        zero = jnp.zeros_like(q)
        q2_ref[0:tq, :] = jnp.where(first, q, zero)
        q2_ref[tq:2 * tq, :] = jnp.where(first, zero, q)
        m_ref[...] = jnp.full(m_ref.shape, NEG_BIG, F32)
        l_ref[...] = jnp.zeros(l_ref.shape, F32)
        acc_ref[...] = jnp.zeros(acc_ref.shape, F32)

    def accumulate(masked):
        s = _nt(q2_ref[...], k_ref[...])
        if masked:
            row = lax.broadcasted_iota(jnp.int32, s.shape, 0)
            col = lax.broadcasted_iota(jnp.int32, s.shape, 1)
            s = jnp.where(col <= jnp.where(row >= tq, row - tq, row), s, NEG_BIG)
        m_prev = m_ref[...]
        m_new = jnp.maximum(m_prev, jnp.max(s, axis=-1, keepdims=True))
        alpha = jnp.exp(m_prev - m_new)
        p = jnp.exp(s - pltpu.repeat(m_new, s.shape[1] // LANES, axis=1))
        l_ref[...] = alpha * l_ref[...] + jnp.sum(p, axis=-1, keepdims=True)
        acc_ref[...] = alpha * acc_ref[...] + _nn(p.astype(BF16), v_ref[...].astype(BF16))
        m_ref[...] = m_new

    @pl.when(ki < qi)
    def _():
        accumulate(False)

    @pl.when(ki == qi)
    def _():
        accumulate(True)
        o = acc_ref[...] / l_ref[...]
        lam = _lambda_value(lam_ref, lam_init)
        y = _subln_gate(o[0:tq], o[tq:2 * tq], lam, sub_ref[...], ga_ref[...], lam_init)
        o_ref[...] = y.astype(BF16)


def _flash_call(q, k, p, a_lambda, subln, bn, s, tq, lam_init):
    nq = s // tq
    vb, gb = COL_AV // LANES, COL_AG // LANES
    return pl.pallas_call(
        functools.partial(_flash_kernel, tq=tq, lam_init=lam_init),
        grid=(bn, A_HEADS, nq, nq),
        in_specs=[pl.BlockSpec((tq, LANES), lambda b, h, i, j: (b * nq + i, h)),
                  pl.BlockSpec((tq, LANES), lambda b, h, i, j: (b * nq + jnp.minimum(i, j), h)),
                  pl.BlockSpec((tq, LANES), lambda b, h, i, j: (b * nq + jnp.minimum(i, j), vb + h)),
                  pl.BlockSpec((tq, LANES), lambda b, h, i, j: (b * nq + i, gb + h)),
                  pl.BlockSpec((4, A_HD), lambda b, h, i, j: (0, 0)),
                  pl.BlockSpec((1, LANES), lambda b, h, i, j: (0, 0))],
        out_specs=pl.BlockSpec((tq, LANES), lambda b, h, i, j: (b * nq + i, h)),
        out_shape=jax.ShapeDtypeStruct((bn * s, BRANCH_W), BF16),
        scratch_shapes=[pltpu.VMEM((2 * tq, LANES), BF16), pltpu.VMEM((2 * tq, LANES), F32),
                        pltpu.VMEM((2 * tq, LANES), F32), pltpu.VMEM((2 * tq, LANES), F32)],
        compiler_params=_cparams(("parallel", "parallel", "parallel", "arbitrary")),
        name="diff_attn_prompt",
    )(q, k, p, p, a_lambda, subln)


PAGES_PER_STEP = 4


def _paged_kernel(pt_ref, q_ref, *refs, n_steps, group, n_new, lam_init):
    del pt_ref
    k_refs, v_refs = refs[0:group], refs[group:2 * group]
    kn_ref, vn_ref, ga_ref, lam_ref, sub_ref, o_ref, s_ref, m_ref, l_ref, acc_ref = refs[2 * group:]
    step = pl.program_id(1)
    rows_h = 2 * n_new

    @pl.when(step == 0)
    def _():
        m_ref[...] = jnp.full(m_ref.shape, NEG_BIG, F32)
        l_ref[...] = jnp.zeros(l_ref.shape, F32)
        acc_ref[...] = jnp.zeros(acc_ref.shape, F32)

    def block(k_list, v_list, is_new):
        width = PAGE_SIZE * len(k_list)
        for j, k_ref in enumerate(k_list):
            for h in range(A_HEADS):
                kh = k_ref[2 * h:2 * h + 2].reshape(2 * A_HD, PAGE_SIZE).astype(BF16)
                s_ref[rows_h * h:rows_h * (h + 1), PAGE_SIZE * j:PAGE_SIZE * (j + 1)] = _nn(q_ref[h], kh)
        s = s_ref[:, 0:width]
        if is_new:
            row = lax.broadcasted_iota(jnp.int32, s.shape, 0)
            col = lax.broadcasted_iota(jnp.int32, s.shape, 1)
            s = jnp.where(col <= (row & (n_new - 1)), s, NEG_BIG)
        m_prev = m_ref[...]
        m_new = jnp.maximum(m_prev, jnp.max(s, axis=-1, keepdims=True))
        alpha = jnp.exp(m_prev - m_new)
        p = jnp.exp(s - pltpu.repeat(m_new, width // LANES, axis=1))
        l_ref[...] = alpha * l_ref[...] + jnp.sum(p, axis=-1, keepdims=True)
        m_ref[...] = m_new
        pb = p.astype(BF16)
        for h in range(A_HEADS):
            rows = slice(rows_h * h, rows_h * (h + 1))
            pv = None
            for j, v_ref in enumerate(v_list):
                vh = v_ref[pl.ds(h, PAGE_SIZE, stride=A_HEADS), :].astype(BF16)
                d = _nn(pb[rows, PAGE_SIZE * j:PAGE_SIZE * (j + 1)], vh)
                pv = d if pv is None else pv + d
            acc_ref[h] = alpha[rows] * acc_ref[h] + pv

    @pl.when(step < n_steps)
    def _():
        block(k_refs, v_refs, False)

    @pl.when(step == n_steps)
    def _():
        block([kn_ref], [vn_ref], True)
        lam = _lambda_value(lam_ref, lam_init)
        linv = 1.0 / l_ref[...]
        for h in range(A_HEADS):
            o = acc_ref[h] * linv[rows_h * h:rows_h * (h + 1)]
            cols = slice(LANES * h, LANES * (h + 1))
            o_ref[:, cols] = _subln_gate(o[0:n_new], o[n_new:rows_h], lam, sub_ref[...], ga_ref[:, cols], lam_init)


def _paged_call(page_table, q_bd, cache_kt, cache_v2, knew, vnew, ga, a_lambda, subln, layer, lam_init):
    bn, n_pages = page_table.shape
    nsub = 2 * A_HEADS
    n_new = q_bd.shape[2] // 2
    group = math.gcd(PAGES_PER_STEP, n_pages)
    n_steps = n_pages // group
    last = n_pages - 1
    page = lambda j: (lambda b, p, pt: (layer, pt[b, jnp.minimum(p * group + j, last)], 0, 0, 0))
    page4 = lambda j: (lambda b, p, pt: (layer, pt[b, jnp.minimum(p * group + j, last)], 0, 0))
    k_specs = [pl.BlockSpec((None, None, nsub, A_HD, PAGE_SIZE), page(j)) for j in range(group)]
    v_specs = [pl.BlockSpec((None, None, PAGE_SIZE * A_HEADS, LANES), page4(j)) for j in range(group)]
    grid_spec = pltpu.PrefetchScalarGridSpec(
        num_scalar_prefetch=1,
        grid=(bn, n_steps + 1),
        in_specs=[pl.BlockSpec((None, A_HEADS, 2 * n_new, LANES), lambda b, p, pt: (b, 0, 0, 0))] + k_specs + v_specs + [
            pl.BlockSpec((None, nsub, A_HD, PAGE_SIZE), lambda b, p, pt: (b, 0, 0, 0)),
            pl.BlockSpec((None, PAGE_SIZE * A_HEADS, LANES), lambda b, p, pt: (b, 0, 0)),
            pl.BlockSpec((None, n_new, BRANCH_W), lambda b, p, pt: (b, 0, 0)),
            pl.BlockSpec((4, A_HD), lambda b, p, pt: (0, 0)),
            pl.BlockSpec((1, LANES), lambda b, p, pt: (0, 0))],
        out_specs=pl.BlockSpec((None, n_new, BRANCH_W), lambda b, p, pt: (b, 0, 0)),
        scratch_shapes=[pltpu.VMEM((nsub * n_new, PAGE_SIZE * group), F32), pltpu.VMEM((nsub * n_new, LANES), F32),
                        pltpu.VMEM((nsub * n_new, LANES), F32), pltpu.VMEM((A_HEADS, 2 * n_new, LANES), F32)],
    )
    return pl.pallas_call(
        functools.partial(_paged_kernel, n_steps=n_steps, group=group, n_new=n_new, lam_init=lam_init),
        grid_spec=grid_spec,
        out_shape=jax.ShapeDtypeStruct((bn, n_new, BRANCH_W), F32),
        compiler_params=_cparams(("parallel", "arbitrary")),
        name="diff_attn_sample",
    )(page_table, q_bd, *([cache_kt] * group), *([cache_v2] * group), knew, vnew, ga, a_lambda, subln)


def _shift_rows(x, prev_row):
    row = lax.broadcasted_iota(jnp.int32, x.shape, 0)
    return jnp.where(row == 0, prev_row, pltpu.roll(x, 1, 0))


def _cumsum_rows(x):
    n = x.shape[0]
    row = lax.broadcasted_iota(jnp.int32, x.shape, 0)
    d = 1
    while d < n:
        x = x + jnp.where(row >= d, pltpu.roll(x, d, 0), 0.0)
        d *= 2
    return x


def _stack2(z):
    first = _first_head(z.shape)
    zero = jnp.zeros_like(z)
    return jnp.concatenate([jnp.where(first, z, zero), jnp.where(first, zero, z)], axis=0)


def _rwkv_kernel(r_ref, k_ref, v_ref, g_ref, lo_ref, sr_ref, sk_ref, sv_ref, sl_ref, st0_ref,
                 mur_ref, muk_ref, muv_ref, mul_ref, w0_ref, a0_ref, kkw_ref, kaw_ref, rk_ref, gnw_ref, gnb_ref,
                 w2_ref, a2_ref, o_ref, st_ref, pr_ref, pk_ref, pv_ref, pl_ref, s_ref,
                 la_ref, rb_ref, v2_ref, bk_ref, x_ref, t_ref, mak_ref, mr_ref, gs_ref, u_ref, *, ln, n_valid):
    c = pl.program_id(1)
    npair = B_HEADS // 2
    l2, l4 = 2 * ln, 4 * ln

    @pl.when(c == 0)
    def _():
        pr_ref[...] = sr_ref[...]
        pk_ref[...] = sk_ref[...]
        pv_ref[...] = sv_ref[...]
        pl_ref[...] = sl_ref[...]
        s_ref[...] = st0_ref[...]

    def mix(x_ref_, prev_ref, mu_ref):
        x = x_ref_[...]
        xm = x + (_shift_rows(x, prev_ref[...]) - x) * mu_ref[...]
        prev_ref[...] = x[n_valid - 1:n_valid]
        return xm

    r = mix(r_ref, pr_ref, mur_ref)
    k = mix(k_ref, pk_ref, muk_ref)
    v = mix(v_ref, pv_ref, muv_ref)
    lo = mix(lo_ref, pl_ref, mul_ref)
    w_pre = w0_ref[...] + _nn(jnp.tanh(lo).astype(BF16), w2_ref[...])
    neg = -w_pre
    softplus = jnp.maximum(neg, 0.0) + jnp.log(1.0 + jnp.exp(-jnp.abs(neg)))
    logw = -jnp.exp(-softplus - 0.5)
    a = _sigmoid(a0_ref[...] + _nn(lo.astype(BF16), a2_ref[...]))
    kk = k * kkw_ref[...]
    k2 = k * (1.0 + (a - 1.0) * kaw_ref[...])
    if n_valid < ln:
        live = lax.broadcasted_iota(jnp.int32, logw.shape, 0) < n_valid
        logw = jnp.where(live, logw, 0.0)
        kk = jnp.where(live, kk, 0.0)
        k2 = jnp.where(live, k2, 0.0)
    cum = _cumsum_rows(logw)
    e_in = jnp.exp(cum)
    e_ex = jnp.exp(cum - logw)
    e_inv = jnp.exp(-cum)
    c_last = cum[ln - 1:ln]
    e_last = jnp.exp(c_last - cum)
    decay_l = jnp.exp(c_last)

    for pr in range(npair):
        cols = slice(LANES * pr, LANES * (pr + 1))
        kkp = kk[:, cols]
        kkp = kkp * lax.rsqrt(_seg_sum(kkp * kkp) + 1e-12)
        b = kkp * a[:, cols]
        k2p = k2[:, cols]
        la_ref[pr] = jnp.concatenate([_stack2(-kkp * e_ex[:, cols]), _stack2(r[:, cols] * e_in[:, cols])], 0).astype(BF16)
        rb_ref[pr] = jnp.concatenate([_stack2(b * e_inv[:, cols]), _stack2(k2p * e_inv[:, cols])], 0).astype(BF16)
        bk_ref[pr] = jnp.concatenate([_stack2(b * e_last[:, cols]), _stack2(k2p * e_last[:, cols])], 0).astype(BF16)
        v2_ref[pr] = _stack2(v[:, cols]).astype(BF16)

    rr = lax.broadcasted_iota(jnp.int32, (l2, l4), 0) & (ln - 1)
    cc_full = lax.broadcasted_iota(jnp.int32, (l2, l4), 1)
    cc = cc_full & (ln - 1)
    strict_right = (cc < rr) & (cc_full >= l2)
    incl = cc <= rr
    rs = lax.broadcasted_iota(jnp.int32, (l2, l2), 0)
    cs = lax.broadcasted_iota(jnp.int32, (l2, l2), 1)
    strict_sq = (cs & (ln - 1)) < (rs & (ln - 1))
    eye = (rs == cs).astype(F32)
    for pr in range(npair):
        la = la_ref[pr]
        mm = _nt(la, rb_ref[pr])
        xa = mm[0:l2]
        x = jnp.where(strict_sq, xa[:, 0:l2], 0.0)
        x_ref[pr] = x
        t_ref[pr] = eye + x
        mak_ref[pr] = jnp.where(strict_right, xa, 0.0).astype(BF16)
        mr_ref[pr] = jnp.where(incl, mm[l2:l4], 0.0).astype(BF16)
        gs_ref[pr] = _nt(la, s_ref[pr].astype(BF16))
    for pr in range(npair):
        v2 = v2_ref[pr]
        u_ref[pr] = gs_ref[pr, 0:l2, :] + _nn(mak_ref[pr], jnp.concatenate([v2, v2], axis=0))

    n = 1
    while 2 * n < ln:
        for pr in range(npair):
            xb = x_ref[pr].astype(BF16)
            x_ref[pr] = _nn(xb, xb)
        for pr in range(npair):
            t = t_ref[pr]
            t_ref[pr] = t + _nn(t.astype(BF16), x_ref[pr].astype(BF16))
        n *= 2

    for pr in range(npair):
        u_ref[pr] = _nn(t_ref[pr].astype(BF16), u_ref[pr].astype(BF16))

    rk_all = r * k2 * rk_ref[...]
    gate = _silu(g_ref[...])
    for pr in range(npair):
        cols = slice(LANES * pr, LANES * (pr + 1))
        u2 = u_ref[pr]
        v2 = v2_ref[pr]
        y2 = gs_ref[pr, l2:l4, :] + _nn(mr_ref[pr], jnp.concatenate([u2.astype(BF16), v2], axis=0))
        y = y2[0:ln] + y2[ln:l2]
        uvt = jnp.concatenate([u2, v2.astype(F32)], axis=0).T.astype(BF16)
        s_ref[pr] = s_ref[pr] * decay_l[:, cols] + _nn(uvt, bk_ref[pr])
        mu = _seg_sum(y) * (1.0 / HALF)
        dy = y - mu
        var = _seg_sum(dy * dy) * (1.0 / HALF)
        yn = dy * lax.rsqrt(var + B_GN_EPS) * gnw_ref[:, cols] + gnb_ref[:, cols]
        yn = yn + _seg_sum(rk_all[:, cols]) * v[:, cols]
        o_ref[:, cols] = (yn * gate[:, cols]).astype(BF16)

    @pl.when(c == pl.num_programs(1) - 1)
    def _():
        st_ref[...] = s_ref[...]


def _rwkv_call(p, lora, shift_r, shift_k, shift_v, shift_l, state_bd, prm, bn, s, ln, n_valid):
    nc = s // ln
    npair = B_HEADS // 2
    rb, kb, vb, gb = (c // BRANCH_W for c in (COL_BR, COL_BK, COL_BV, COL_BG))
    tok = lambda cb: pl.BlockSpec((ln, BRANCH_W), lambda b, c: (b * nc + c, cb))
    per_b = lambda w: pl.BlockSpec((None, 1, w), lambda b, c: (b, 0, 0))
    row = lambda w: pl.BlockSpec((1, w), lambda b, c: (0, 0))
    st_spec = pl.BlockSpec((None, npair, LANES, LANES), lambda b, c: (b, 0, 0, 0))
    l2, l4 = 2 * ln, 4 * ln
    return pl.pallas_call(
        functools.partial(_rwkv_kernel, ln=ln, n_valid=n_valid),
        grid=(bn, nc),
        in_specs=[tok(rb), tok(kb), tok(vb), tok(gb),
                  pl.BlockSpec((ln, LANES), lambda b, c: (b * nc + c, 0)),
                  per_b(BRANCH_W), per_b(BRANCH_W), per_b(BRANCH_W), per_b(LANES), st_spec,
                  row(BRANCH_W), row(BRANCH_W), row(BRANCH_W), row(LANES),
                  row(BRANCH_W), row(BRANCH_W), row(BRANCH_W), row(BRANCH_W), row(BRANCH_W), row(BRANCH_W),
                  row(BRANCH_W),
                  pl.BlockSpec((LANES, BRANCH_W), lambda b, c: (0, 0)),
                  pl.BlockSpec((LANES, BRANCH_W), lambda b, c: (0, 0))],
        out_specs=[pl.BlockSpec((ln, BRANCH_W), lambda b, c: (b * nc + c, 0)), st_spec],
        out_shape=[jax.ShapeDtypeStruct((bn * s, BRANCH_W), BF16),
                   jax.ShapeDtypeStruct((bn, npair, LANES, LANES), F32)],
        scratch_shapes=[pltpu.VMEM((1, BRANCH_W), F32), pltpu.VMEM((1, BRANCH_W), F32),
                        pltpu.VMEM((1, BRANCH_W), F32), pltpu.VMEM((1, LANES), F32),
                        pltpu.VMEM((npair, LANES, LANES), F32),
                        pltpu.VMEM((npair, l4, LANES), BF16), pltpu.VMEM((npair, l4, LANES), BF16),
                        pltpu.VMEM((npair, l2, LANES), BF16), pltpu.VMEM((npair, l4, LANES), BF16),
                        pltpu.VMEM((npair, l2, l2), F32), pltpu.VMEM((npair, l2, l2), F32),
                        pltpu.VMEM((npair, l2, l4), BF16), pltpu.VMEM((npair, l2, l4), BF16),
                        pltpu.VMEM((npair, l4, LANES), F32), pltpu.VMEM((npair, l2, LANES), F32)],
        compiler_params=_cparams(("parallel", "arbitrary")),
        name="rwkv7_chunked",
    )(p, p, p, p, lora, shift_r, shift_k, shift_v, shift_l, state_bd,
      prm["mu_r"], prm["mu_k"], prm["mu_v"], prm["mu_l"], prm["w0"], prm["a0"], prm["kk"], prm["ka"], prm["rk"],
      prm["gn_w"], prm["gn_b"], prm["w2"], prm["a2"])


def _ret_kernel(q_ref, k_ref, v_ref, g_ref, cos_ref, sin_ref, st0_ref, o_ref, st_ref, s_ref, *, cn, n_valid):
    c = pl.program_id(1)

    @pl.when(c == 0)
    def _():
        s_ref[...] = st0_ref[...]

    cos = cos_ref[...]
    sin = sin_ref[...]
    rowc = lax.broadcasted_iota(jnp.int32, (cn, cn), 0)
    colc = lax.broadcasted_iota(jnp.int32, (cn, cn), 1)
    dist = (rowc - colc).astype(F32)
    causal = colc <= rowc
    n_idx = lax.broadcasted_iota(jnp.int32, (cn, 1), 0).astype(F32)
    first128 = _first_head((1, LANES))
    first256 = lax.broadcasted_iota(jnp.int32, (1, 2 * LANES), 1) < LANES
    ri = lax.broadcasted_iota(jnp.int32, (LANES, 2 * LANES), 0) < HALF
    ci = lax.broadcasted_iota(jnp.int32, (LANES, 2 * LANES), 1) < LANES
    live = lax.broadcasted_iota(jnp.int32, (cn, 1), 0) < n_valid
    for pr in range(C_HEADS // 2):
        lg = [math.log(1.0 - 2.0 ** (-5.0 - (2 * pr + j))) for j in range(2)]
        qc = slice(LANES * pr, LANES * (pr + 1))
        vc = slice(2 * LANES * pr, 2 * LANES * (pr + 1))
        q = _rope(q_ref[:, qc], cos, sin)
        k = _rope(k_ref[:, qc], cos, sin) * (C_KD ** -0.5)
        if n_valid < cn:
            k = jnp.where(live, k, 0.0)
        v = v_ref[:, vc]
        vb = v.astype(BF16)
        kb = k.astype(BF16)
        zero = jnp.zeros_like(q)
        outs = []
        for j in range(2):
            qj = jnp.where(first128, q, zero) if j == 0 else jnp.where(first128, zero, q)
            sc = _nt(qj.astype(BF16), kb)
            sc = sc * jnp.where(causal, jnp.exp(jnp.maximum(dist, 0.0) * lg[j]), 0.0)
            outs.append(_nn(sc.astype(BF16), vb[:, LANES * j:LANES * (j + 1)]))
        st = s_ref[pr]
        xi = jnp.where(first256, jnp.exp((n_idx + 1.0) * lg[0]), jnp.exp((n_idx + 1.0) * lg[1]))
        o = jnp.concatenate(outs, axis=1) + _nn(q.astype(BF16), st.astype(BF16)) * xi
        zeta = jnp.where(first128, jnp.exp((n_valid - 1.0 - n_idx) * lg[0]), jnp.exp((n_valid - 1.0 - n_idx) * lg[1]))
        upd = _nn((k * zeta).T.astype(BF16), vb)
        gch = jnp.where(first256, math.exp(n_valid * lg[0]), math.exp(n_valid * lg[1]))
        s_ref[pr] = st * gch + jnp.where(ri == ci, upd, 0.0)
        gate = _silu(g_ref[:, vc])
        for j in range(2):
            oj = o[:, LANES * j:LANES * (j + 1)]
            oj = oj * lax.rsqrt(jnp.mean(oj * oj, axis=-1, keepdims=True) + NORM_EPS)
            o_ref[:, 2 * LANES * pr + LANES * j:2 * LANES * pr + LANES * (j + 1)] = (
                oj * gate[:, LANES * j:LANES * (j + 1)]).astype(BF16)

    @pl.when(c == pl.num_programs(1) - 1)
    def _():
        st_ref[...] = s_ref[...]


def _ret_call(p, cos, sin, state_bd, bn, s, cn, n_valid):
    nc = s // cn
    npair = C_HEADS // 2
    ntab = cos.shape[0] // cn
    st_spec = pl.BlockSpec((None, npair, LANES, 2 * LANES), lambda b, c: (b, 0, 0, 0))
    return pl.pallas_call(
        functools.partial(_ret_kernel, cn=cn, n_valid=n_valid),
        grid=(bn, nc),
        in_specs=[pl.BlockSpec((cn, 512), lambda b, c: (b * nc + c, COL_CQ // 512)),
                  pl.BlockSpec((cn, 512), lambda b, c: (b * nc + c, COL_CK // 512)),
                  pl.BlockSpec((cn, BRANCH_W), lambda b, c: (b * nc + c, COL_CV // BRANCH_W)),
                  pl.BlockSpec((cn, BRANCH_W), lambda b, c: (b * nc + c, COL_CG // BRANCH_W)),
                  pl.BlockSpec((cn, LANES), lambda b, c: (c % ntab, 0)),
                  pl.BlockSpec((cn, LANES), lambda b, c: (c % ntab, 0)),
                  st_spec],
        out_specs=[pl.BlockSpec((cn, BRANCH_W), lambda b, c: (b * nc + c, 0)), st_spec],
        out_shape=[jax.ShapeDtypeStruct((bn * s, BRANCH_W), BF16),
                   jax.ShapeDtypeStruct((bn, npair, LANES, 2 * LANES), F32)],
        scratch_shapes=[pltpu.VMEM((npair, LANES, 2 * LANES), F32)],
        compiler_params=_cparams(("parallel", "arbitrary")),
        name="retention_chunkwise",
    )(p, p, p, p, cos, sin, state_bd)


def _merge_kernel(oa_ref, ob_ref, oc_ref, wb_ref, g0_ref, g1_ref, g2_ref, o_ref):
    acc = _sigmoid(g0_ref[...]) * _nn(oa_ref[...], wb_ref[0])
    acc = acc + _sigmoid(g1_ref[...]) * _nn(ob_ref[...], wb_ref[1])
    acc = acc + _sigmoid(g2_ref[...]) * _nn(oc_ref[...], wb_ref[2])
    o_ref[...] = acc.astype(BF16)


def _merge_call(oa, ob, oc, wb, layer, p, tm, tn):
    m = oa.shape[0]
    nj = D_MODEL // tn
    act = pl.BlockSpec((tm, BRANCH_W), lambda i, j: (i, 0))
    gate = lambda n: pl.BlockSpec((tm, tn), lambda i, j: (i, (COL_GATE + n * D_MODEL) // tn + j))
    return pl.pallas_call(
        _merge_kernel,
        grid=(m // tm, nj),
        in_specs=[act, act, act, pl.BlockSpec((None, 3, BRANCH_W, tn), lambda i, j: (layer, 0, 0, j)), gate(0), gate(1), gate(2)],
        out_specs=pl.BlockSpec((tm, tn), lambda i, j: (i, j)),
        out_shape=jax.ShapeDtypeStruct((m, D_MODEL), BF16),
        compiler_params=_cparams(("parallel", "arbitrary")),
        name="branch_merge",
    )(oa, ob, oc, wb, p, p, p)


def _out_kernel(a_ref, b_ref, x_ref, o_ref):
    o_ref[...] = x_ref[...] + _nn(a_ref[...], b_ref[...])


def _out_call(merged, w_out, layer, x2d, tm, tn):
    m = merged.shape[0]
    return pl.pallas_call(
        _out_kernel,
        grid=(m // tm, D_MODEL // tn),
        in_specs=[pl.BlockSpec((tm, D_MODEL), lambda i, j: (i, 0)), pl.BlockSpec((None, D_MODEL, tn), lambda i, j: (layer, 0, j)),
                  pl.BlockSpec((tm, tn), lambda i, j: (i, j))],
        out_specs=pl.BlockSpec((tm, tn), lambda i, j: (i, j)),
        out_shape=jax.ShapeDtypeStruct((m, D_MODEL), F32),
        compiler_params=_cparams(("parallel", "arbitrary")),
        name="out_proj_residual",
    )(merged, w_out, x2d)


def _rope_tables(pos, inv_freq):
    ang = pos.astype(F32)[:, None] * inv_freq[None, :]
    cos = jnp.cos(ang)
    sin = jnp.sin(ang)
    cos = jnp.concatenate([cos, cos, cos, cos], axis=1)
    sin = jnp.concatenate([-sin, sin, -sin, sin], axis=1)
    return cos, sin


def _permute_w_in(w):
    b0 = 4096
    c0 = b0 + B_SHIFT + BRANCH_W
    cut = lambda lo, hi: w[..., lo:hi]
    main = jnp.concatenate([cut(0, b0 + 1024), cut(b0 + 1088, b0 + 3136), cut(b0 + 3200, b0 + 4224),
                            cut(c0, c0 + 3072 + 3 * D_MODEL)], axis=-1)
    lora = jnp.concatenate([cut(b0 + 1024, b0 + 1088), cut(b0 + 3136, b0 + 3200)], axis=-1)
    return main.astype(BF16), lora.astype(BF16)


def _split_shift(sh):
    return (sh[..., 0:1024], sh[..., 1088:2112], sh[..., 2112:3136],
            jnp.concatenate([sh[..., 1024:1088], sh[..., 3136:3200]], axis=-1))


def _join_shift(r, k, v, lo):
    return jnp.concatenate([r, lo[..., 0:64], k, v, lo[..., 64:128]], axis=-1)


def _pairs_to_blockdiag(st):
    bn, nh, r, c = st.shape
    st = st.reshape(bn, nh // 2, 2, r, c)
    z = jnp.zeros_like(st[:, :, 0])
    top = jnp.concatenate([st[:, :, 0], z], axis=-1)
    bot = jnp.concatenate([z, st[:, :, 1]], axis=-1)
    return jnp.concatenate([top, bot], axis=-2)


def _blockdiag_to_pairs(bd):
    bn, npair, r2, c2 = bd.shape
    r, c = r2 // 2, c2 // 2
    return jnp.stack([bd[:, :, :r, :c], bd[:, :, r:, c:]], axis=2).reshape(bn, 2 * npair, r, c)


def _layer_params(l, w_main, w_lora, w_branch_bf, w_out_bf, norm_w, a_qnorm, a_knorm, a_lambda, a_subln, b_mu, b_w0,
                  b_w2, b_a0, b_a2, b_kk, b_ka, b_rk, b_gn_w, b_gn_b):
    mu_r, mu_k, mu_v, mu_l = _split_shift(b_mu[l])
    zeros = jnp.zeros((B_LORA, BRANCH_W), F32)
    row = lambda t: t.reshape(1, -1)
    return dict(
        layer=l, w_main=w_main, w_lora=w_lora, w_branch=w_branch_bf, w_out=w_out_bf, norm_w=row(norm_w[l]),
        qn=row(jnp.tile(a_qnorm[l], 2)), kn=row(jnp.tile(a_knorm[l], 2)), a_lambda=a_lambda[l], subln=row(a_subln[l]),
        lam_init=0.8 - 0.6 * math.exp(-0.3 * l),
        rwkv=dict(mu_r=row(mu_r), mu_k=row(mu_k), mu_v=row(mu_v), mu_l=row(mu_l), w0=row(b_w0[l]), a0=row(b_a0[l]),
                  kk=row(b_kk[l]), ka=row(b_ka[l]), rk=row(b_rk[l]), gn_w=row(b_gn_w[l]), gn_b=row(b_gn_b[l]),
                  w2=jnp.concatenate([b_w2[l], zeros], axis=0).astype(BF16),
                  a2=jnp.concatenate([zeros, b_a2[l]], axis=0).astype(BF16)))


def _tiles(m):
    return min(m, 1024)


def _trunk_layer(x2d, bn, s, n_valid, lp, tabs_a, tabs_c, attend, shift0, rwkv0_bd, ret0_bd, chunk_b, chunk_c):
    m = bn * s
    tm = _tiles(m)
    h = _rms_call(x2d, lp["norm_w"], min(m, 512))
    p = _mm_call(h, lp["w_main"], lp["layer"], tm, 1024, "in_proj")
    lora = _mm_call(h, lp["w_lora"], lp["layer"], tm, LANES, "in_proj_lora")
    t_qk = min(tabs_a[0].shape[0], 512)
    q_bf, k_f32, k_bf = _qk_call(p, lp["qn"], lp["kn"], tabs_a[0], tabs_a[1], t_qk)
    oa = attend(q_bf, k_f32, k_bf, p)
    sr, sk, sv, sl = shift0
    ob, rwkv_bd = _rwkv_call(p, lora, sr, sk, sv, sl, rwkv0_bd, lp["rwkv"], bn, s, chunk_b, min(n_valid, chunk_b))
    oc, ret_bd = _ret_call(p, tabs_c[0], tabs_c[1], ret0_bd, bn, s, chunk_c, min(n_valid, chunk_c))
    merged = _merge_call(oa, ob, oc, lp["w_branch"], lp["layer"], p, min(m, 512), 1024)
    y = _out_call(merged, lp["w_out"], lp["layer"], x2d, tm, 1024)
    return y, k_f32, p, lora, rwkv_bd, ret_bd


def kernel(x_prompt, x_sample, cache_k, cache_v, page_table, state_rwkv, state_shift, state_ret, norm_w, w_in, a_qnorm, a_knorm, a_lambda, a_subln, b_mu, b_w0, b_w2, b_a0, b_a2, b_kk, b_ka, b_rk, b_gn_w, b_gn_b, w_branch, w_out):
    bp, sp, _ = x_prompt.shape
    bs, ss, _ = x_sample.shape
    ss_pad = 16
    depth = w_in.shape[0]
    n_pool = cache_k.shape[1]

    w_main, w_lora = _permute_w_in(w_in)
    w_branch_bf = w_branch.astype(BF16)
    w_out_bf = w_out.astype(BF16)
    cache_kt = jnp.transpose(cache_k, (0, 1, 3, 4, 2))
    cache_v2 = cache_v.reshape(depth, n_pool, PAGE_SIZE * A_HEADS, LANES)

    inv_a = ROPE_THETA ** (-jnp.arange(A_HD // 2, dtype=F32) / (A_HD // 2))
    inv_c = ROPE_THETA ** (-jnp.linspace(0.0, 1.0, C_KD // 2, dtype=F32))
    pos_p = jnp.arange(sp)
    pos_s = PAST_LEN + jnp.arange(ss_pad)
    tabs_a_p, tabs_c_p = _rope_tables(pos_p, inv_a), _rope_tables(pos_p, inv_c)
    tabs_a_s = tuple(jnp.tile(t, (bs, 1)) for t in _rope_tables(pos_s, inv_a))
    tabs_c_s = _rope_tables(pos_s, inv_c)

    xp = x_prompt.reshape(bp * sp, D_MODEL)
    xs = jnp.pad(x_sample, ((0, 0), (0, ss_pad - ss), (0, 0))).reshape(bs * ss_pad, D_MODEL)
    zero_shift = (jnp.zeros((bp, 1, BRANCH_W), F32),) * 3 + (jnp.zeros((bp, 1, LANES), F32),)
    rwkv0_p = jnp.zeros((bp, B_HEADS // 2, LANES, LANES), F32)
    ret0_p = jnp.zeros((bp, C_HEADS // 2, LANES, 2 * LANES), F32)

    outs = [[] for _ in range(10)]
    for l in range(depth):
        lp = _layer_params(l, w_main, w_lora, w_branch_bf, w_out_bf, norm_w, a_qnorm, a_knorm, a_lambda, a_subln, b_mu,
                           b_w0, b_w2, b_a0, b_a2, b_kk, b_ka, b_rk, b_gn_w, b_gn_b)

        def attend_p(q_bf, k_f32, k_bf, p, lp=lp):
            return _flash_call(q_bf, k_bf, p, lp["a_lambda"], lp["subln"], bp, sp, min(sp, 512), lp["lam_init"])

        xp, k_new, p, lora, rw, rt = _trunk_layer(xp, bp, sp, sp, lp, tabs_a_p, tabs_c_p, attend_p, zero_shift,
                                                  rwkv0_p, ret0_p, 64, 128)
        last = p.reshape(bp, sp, N_MAIN)[:, sp - 1]
        outs[0].append(k_new.reshape(bp, sp, 2 * A_HEADS, A_HD))
        outs[1].append(p.reshape(bp, sp, N_MAIN)[:, :, COL_AV:COL_AV + BRANCH_W].reshape(bp, sp, A_HEADS, LANES))
        outs[2].append(_blockdiag_to_pairs(rw))
        outs[3].append(_join_shift(last[:, COL_BR:COL_BR + 1024], last[:, COL_BK:COL_BK + 1024],
                                   last[:, COL_BV:COL_BV + 1024], lora.reshape(bp, sp, LANES)[:, sp - 1]))
        outs[4].append(_blockdiag_to_pairs(rt))

        def attend_s(q_bf, k_f32, k_bf, p, lp=lp, l=l):
            q5 = q_bf.reshape(bs, ss_pad, A_HEADS, 2, A_HD)[:, :ss].transpose(0, 2, 3, 1, 4)
            zq = jnp.zeros_like(q5[:, :, 0])
            q_bd = jnp.concatenate([jnp.concatenate([q5[:, :, 0], zq], axis=-1),
                                    jnp.concatenate([zq, q5[:, :, 1]], axis=-1)], axis=2)
            kn = k_f32.reshape(bs, ss_pad, 2 * A_HEADS, A_HD)[:, :ss].transpose(0, 2, 3, 1)
            kn = jnp.pad(kn, ((0, 0), (0, 0), (0, 0), (0, PAGE_SIZE - ss)))
            p3 = p.reshape(bs, ss_pad, N_MAIN)[:, :ss]
            vn = p3[:, :, COL_AV:COL_AV + BRANCH_W].reshape(bs, ss, A_HEADS, LANES)
            vn = jnp.pad(vn, ((0, 0), (0, PAGE_SIZE - ss), (0, 0), (0, 0))).reshape(bs, PAGE_SIZE * A_HEADS, LANES)
            ga = p3[:, :, COL_AG:COL_AG + BRANCH_W]
            o = _paged_call(page_table, q_bd, cache_kt, cache_v2, kn, vn, ga, lp["a_lambda"], lp["subln"], l,
                            lp["lam_init"])
            return jnp.pad(o, ((0, 0), (0, ss_pad - ss), (0, 0))).astype(BF16).reshape(bs * ss_pad, BRANCH_W)

        sh = tuple(t[:, None, :] for t in _split_shift(state_shift[l]))
        rwkv0_s = _pairs_to_blockdiag(state_rwkv[l])
        ret0_s = _pairs_to_blockdiag(state_ret[l])
        xs, k_new, p, lora, rw, rt = _trunk_layer(xs, bs, ss_pad, ss, lp, tabs_a_s, tabs_c_s, attend_s, sh,
                                                  rwkv0_s, ret0_s, ss_pad, ss_pad)
        p3 = p.reshape(bs, ss_pad, N_MAIN)
        last = p3[:, ss - 1]
        outs[5].append(k_new.reshape(bs, ss_pad, 2 * A_HEADS, A_HD)[:, :ss])
        outs[6].append(p3[:, :ss, COL_AV:COL_AV + BRANCH_W].reshape(bs, ss, A_HEADS, LANES))
        outs[7].append(_blockdiag_to_pairs(rw))
        outs[8].append(_join_shift(last[:, COL_BR:COL_BR + 1024], last[:, COL_BK:COL_BK + 1024],
                                   last[:, COL_BV:COL_BV + 1024], lora.reshape(bs, ss_pad, LANES)[:, ss - 1]))
        outs[9].append(_blockdiag_to_pairs(rt))

    st = [jnp.stack(o) for o in outs]
    y_prompt = xp.reshape(bp, sp, D_MODEL)
    y_sample = xs.reshape(bs, ss_pad, D_MODEL)[:, :ss]
    return (y_prompt, y_sample, st[0], st[1], st[2], st[3], st[4], st[5], st[6], st[7], st[8], st[9])
```

```python
import functools
import math

import jax
import jax.numpy as jnp
from jax import lax
from jax.experimental import pallas as pl
from jax.experimental.pallas import tpu as pltpu

F32 = jnp.float32
BF16 = jnp.bfloat16

D_MODEL = 2048
DEPTH = 4
PAST_LEN = 16384
PAGE_SIZE = 128
BRANCH_W = 1024
A_HEADS = 8
A_HD = 64
B_HEADS = 16
B_LORA = 64
B_SHIFT = 3 * BRANCH_W + 2 * B_LORA
B_GN_EPS = 64e-5
C_HEADS = 8
C_KD = 64
ROPE_THETA = 10000.0
NORM_EPS = 1e-6
LANES = 128
HALF = 64
NEG_BIG = -1e30

COL_AQ, COL_AK, COL_AV, COL_AG = 0, 1024, 2048, 3072
COL_BR, COL_BK, COL_BV, COL_BG = 4096, 5120, 6144, 7168
COL_CQ, COL_CK, COL_CV, COL_CG = 8192, 8704, 9216, 10240
COL_GATE = 11264
N_MAIN = 17408
VMEM_LIMIT = 56 * 1024 * 1024


def _cparams(sem):
    return pltpu.CompilerParams(dimension_semantics=sem, vmem_limit_bytes=VMEM_LIMIT)


def _nt(a, b):
    return lax.dot_general(a, b, (((1,), (1,)), ((), ())), preferred_element_type=F32)


def _nn(a, b):
    return jnp.dot(a, b, preferred_element_type=F32)


def _silu(x):
    return x * (1.0 / (1.0 + jnp.exp(-x)))


def _sigmoid(x):
    return 1.0 / (1.0 + jnp.exp(-x))


def _tile_lanes(x, n):
    return x if n == 1 else jnp.concatenate([x] * n, axis=1)


def _first_head(shape):
    return lax.broadcasted_iota(jnp.int32, shape, len(shape) - 1) < HALF


def _seg_sum(x):
    first = _first_head(x.shape)
    zero = jnp.zeros_like(x)
    sa = jnp.sum(jnp.where(first, x, zero), axis=-1, keepdims=True)
    sb = jnp.sum(jnp.where(first, zero, x), axis=-1, keepdims=True)
    return jnp.where(first, sa, sb)


def _rope(x, cos, sin_signed):
    lane = lax.broadcasted_iota(jnp.int32, x.shape, 1)
    first_half = (lane % HALF) < (HALF // 2)
    partner = jnp.where(first_half, pltpu.roll(x, LANES - HALF // 2, 1), pltpu.roll(x, HALF // 2, 1))
    return x * cos + partner * sin_signed


def _rms_kernel(x_ref, w_ref, o_ref):
    x = x_ref[...]
    y = x * lax.rsqrt(jnp.mean(x * x, axis=-1, keepdims=True) + NORM_EPS)
    o_ref[...] = (y * w_ref[...]).astype(BF16)


def _rms_call(x2d, w, tm):
    m = x2d.shape[0]
    return pl.pallas_call(
        _rms_kernel,
        grid=(m // tm,),
        in_specs=[pl.BlockSpec((tm, D_MODEL), lambda i: (i, 0)), pl.BlockSpec((1, D_MODEL), lambda i: (0, 0))],
        out_specs=pl.BlockSpec((tm, D_MODEL), lambda i: (i, 0)),
        out_shape=jax.ShapeDtypeStruct((m, D_MODEL), BF16),
        compiler_params=_cparams(("parallel",)),
        name="rmsnorm",
    )(x2d, w)


def _mm_kernel(a_ref, b_ref, o_ref):
    o_ref[...] = _nn(a_ref[...], b_ref[...])


def _mm_call(a, b, layer, tm, tn, name):
    m, k = a.shape
    n = b.shape[2]
    return pl.pallas_call(
        _mm_kernel,
        grid=(m // tm, n // tn),
        in_specs=[pl.BlockSpec((tm, k), lambda i, j: (i, 0)), pl.BlockSpec((None, k, tn), lambda i, j: (layer, 0, j))],
        out_specs=pl.BlockSpec((tm, tn), lambda i, j: (i, j)),
        out_shape=jax.ShapeDtypeStruct((m, n), F32),
        compiler_params=_cparams(("parallel", "arbitrary")),
        name=name,
    )(a, b)


def _qk_kernel(p_ref, qn_ref, kn_ref, cos_ref, sin_ref, q_ref, k32_ref, kbf_ref):
    cos = cos_ref[...]
    sin = sin_ref[...]

    def norm_rope(x, nw):
        ms = _seg_sum(x * x) * (1.0 / HALF)
        return _rope(x * lax.rsqrt(ms + NORM_EPS) * nw, cos, sin)

    for c in range(BRANCH_W // LANES):
        cols = slice(LANES * c, LANES * (c + 1))
        q_ref[:, cols] = (norm_rope(p_ref[:, cols], qn_ref[...]) * (A_HD ** -0.5)).astype(BF16)
        kc = norm_rope(p_ref[:, BRANCH_W + LANES * c:BRANCH_W + LANES * (c + 1)], kn_ref[...])
        k32_ref[:, cols] = kc
        kbf_ref[:, cols] = kc.astype(BF16)


def _qk_call(p, qn, kn, cos, sin, tm):
    m = p.shape[0]
    ntab = cos.shape[0] // tm
    blk = pl.BlockSpec((tm, BRANCH_W), lambda i: (i, 0))
    return pl.pallas_call(
        _qk_kernel,
        grid=(m // tm,),
        in_specs=[pl.BlockSpec((tm, 2 * BRANCH_W), lambda i: (i, 0)),
                  pl.BlockSpec((1, LANES), lambda i: (0, 0)), pl.BlockSpec((1, LANES), lambda i: (0, 0)),
                  pl.BlockSpec((tm, LANES), lambda i: (i % ntab, 0)), pl.BlockSpec((tm, LANES), lambda i: (i % ntab, 0))],
        out_specs=[blk, blk, blk],
        out_shape=[jax.ShapeDtypeStruct((m, BRANCH_W), BF16), jax.ShapeDtypeStruct((m, BRANCH_W), F32),
                   jax.ShapeDtypeStruct((m, BRANCH_W), BF16)],
        compiler_params=_cparams(("parallel",)),
        name="qk_norm_rope",
    )(p, qn, kn, cos, sin)


def _lambda_value(lam_ref, lam_init):
    lv = lam_ref[...]
    s1 = jnp.sum(lv[0:1] * lv[1:2], axis=-1, keepdims=True)
    s2 = jnp.sum(lv[2:3] * lv[3:4], axis=-1, keepdims=True)
    return jnp.exp(s1) - jnp.exp(s2) + lam_init


def _subln_gate(o1, o2, lam, sub, ga, lam_init):
    a = o1 - lam * o2
    y = a * lax.rsqrt(jnp.mean(a * a, axis=-1, keepdims=True) + NORM_EPS) * sub
    return (y * (1.0 - lam_init)) * _silu(ga)


def _flash_kernel(qi_ref, ki_ref, q_ref, k_ref, v_ref, ga_ref, lam_ref, sub_ref, o_ref, q2_ref, m_ref, l_ref, acc_ref, *,
                  tq, lam_init):
    step = pl.program_id(2)
    qi = qi_ref[step]
    ki = ki_ref[step]

    @pl.when(ki == 0)
    def _():
        q = q_ref[...]
        first = _first_head(q.shape)
        zero = jnp.zeros_like(q)
        q2_ref[0:tq, :] = jnp.where(first, q, zero)
        q2_ref[tq:2 * tq, :] = jnp.where(first, zero, q)
        m_ref[...] = jnp.full(m_ref.shape, NEG_BIG, F32)
        l_ref[...] = jnp.zeros(l_ref.shape, F32)
        acc_ref[...] = jnp.zeros(acc_ref.shape, F32)

    def accumulate(masked):
        s = _nt(q2_ref[...], k_ref[...])
        if masked:
            row = lax.broadcasted_iota(jnp.int32, s.shape, 0)
            col = lax.broadcasted_iota(jnp.int32, s.shape, 1)
            s = jnp.where(col <= jnp.where(row >= tq, row - tq, row), s, NEG_BIG)
        m_prev = m_ref[...]
        m_new = jnp.maximum(m_prev, jnp.max(s, axis=-1, keepdims=True))
        alpha = jnp.exp(m_prev - m_new)
        p = jnp.exp(s - _tile_lanes(m_new, s.shape[1] // LANES))
        l_ref[...] = alpha * l_ref[...] + jnp.sum(p, axis=-1, keepdims=True)
        acc_ref[...] = alpha * acc_ref[...] + _nn(p.astype(BF16), v_ref[...].astype(BF16))
        m_ref[...] = m_new

    @pl.when(ki < qi)
    def _():
        accumulate(False)

    @pl.when(ki == qi)
    def _():
        accumulate(True)
        o = acc_ref[...] / l_ref[...]
        lam = _lambda_value(lam_ref, lam_init)
        y = _subln_gate(o[0:tq], o[tq:2 * tq], lam, sub_ref[...], ga_ref[...], lam_init)
        o_ref[...] = y.astype(BF16)


def _flash_call(q, k, p, a_lambda, subln, bn, s, tq, lam_init):
    nq = s // tq
    vb, gb = COL_AV // LANES, COL_AG // LANES
    pairs = [(i, j) for i in range(nq) for j in range(i + 1)]
    qi_tab = jnp.asarray([i for i, _ in pairs], jnp.int32)
    ki_tab = jnp.asarray([j for _, j in pairs], jnp.int32)
    grid_spec = pltpu.PrefetchScalarGridSpec(
        num_scalar_prefetch=2,
        grid=(bn, A_HEADS, len(pairs)),
        in_specs=[pl.BlockSpec((tq, LANES), lambda b, h, t, qi, ki: (b * nq + qi[t], h)),
                  pl.BlockSpec((tq, LANES), lambda b, h, t, qi, ki: (b * nq + ki[t], h)),
                  pl.BlockSpec((tq, LANES), lambda b, h, t, qi, ki: (b * nq + ki[t], vb + h)),
                  pl.BlockSpec((tq, LANES), lambda b, h, t, qi, ki: (b * nq + qi[t], gb + h)),
                  pl.BlockSpec((4, A_HD), lambda b, h, t, qi, ki: (0, 0)),
                  pl.BlockSpec((1, LANES), lambda b, h, t, qi, ki: (0, 0))],
        out_specs=pl.BlockSpec((tq, LANES), lambda b, h, t, qi, ki: (b * nq + qi[t], h)),
        scratch_shapes=[pltpu.VMEM((2 * tq, LANES), BF16), pltpu.VMEM((2 * tq, LANES), F32),
                        pltpu.VMEM((2 * tq, LANES), F32), pltpu.VMEM((2 * tq, LANES), F32)],
    )
    return pl.pallas_call(
        functools.partial(_flash_kernel, tq=tq, lam_init=lam_init),
        grid_spec=grid_spec,
        out_shape=jax.ShapeDtypeStruct((bn * s, BRANCH_W), BF16),
        compiler_params=_cparams(("parallel", "parallel", "arbitrary")),
        name="diff_attn_prompt",
    )(qi_tab, ki_tab, q, k, p, p, a_lambda, subln)


PAGES_PER_STEP = 8


def _paged_kernel(pt_ref, q_ref, *refs, n_steps, group, n_new, lam_init):
    del pt_ref
    k_refs, v_refs = refs[0:group], refs[group:2 * group]
    kn_ref, vn_ref, ga_ref, lam_ref, sub_ref, o_ref, s_ref, m_ref, l_ref, acc_ref = refs[2 * group:]
    step = pl.program_id(1)
    rows_h = 2 * n_new

    @pl.when(step == 0)
    def _():
        m_ref[...] = jnp.full(m_ref.shape, NEG_BIG, F32)
        l_ref[...] = jnp.zeros(l_ref.shape, F32)
        acc_ref[...] = jnp.zeros(acc_ref.shape, F32)

    def block(k_list, v_list, is_new):
        width = PAGE_SIZE * len(k_list)
        for j, k_ref in enumerate(k_list):
            for h in range(A_HEADS):
                kh = k_ref[2 * h:2 * h + 2].reshape(2 * A_HD, PAGE_SIZE).astype(BF16)
                s_ref[rows_h * h:rows_h * (h + 1), PAGE_SIZE * j:PAGE_SIZE * (j + 1)] = _nn(q_ref[h], kh)
        s = s_ref[:, 0:width]
        if is_new:
            row = lax.broadcasted_iota(jnp.int32, s.shape, 0)
            col = lax.broadcasted_iota(jnp.int32, s.shape, 1)
            s = jnp.where(col <= (row & (n_new - 1)), s, NEG_BIG)
        m_prev = m_ref[...]
        m_new = jnp.maximum(m_prev, jnp.max(s, axis=-1, keepdims=True))
        alpha = jnp.exp(m_prev - m_new)
        p = jnp.exp(s - _tile_lanes(m_new, width // LANES))
        l_ref[...] = alpha * l_ref[...] + jnp.sum(p, axis=-1, keepdims=True)
        m_ref[...] = m_new
        pb = p.astype(BF16)
        for h in range(A_HEADS):
            rows = slice(rows_h * h, rows_h * (h + 1))
            pv = None
            for j, v_ref in enumerate(v_list):
                vh = v_ref[pl.ds(h, PAGE_SIZE, stride=A_HEADS), :].astype(BF16)
                d = _nn(pb[rows, PAGE_SIZE * j:PAGE_SIZE * (j + 1)], vh)
                pv = d if pv is None else pv + d
            acc_ref[h] = alpha[rows] * acc_ref[h] + pv

    @pl.when(step < n_steps)
    def _():
        block(k_refs, v_refs, False)

    @pl.when(step == n_steps)
    def _():
        block([kn_ref], [vn_ref], True)
        lam = _lambda_value(lam_ref, lam_init)
        linv = 1.0 / l_ref[...]
        for h in range(A_HEADS):
            o = acc_ref[h] * linv[rows_h * h:rows_h * (h + 1)]
            cols = slice(LANES * h, LANES * (h + 1))
            o_ref[:, cols] = _subln_gate(o[0:n_new], o[n_new:rows_h], lam, sub_ref[...], ga_ref[:, cols], lam_init)


def _paged_call(page_table, q_bd, cache_kt, cache_v2, knew, vnew, ga, a_lambda, subln, layer, lam_init):
    bn, n_pages = page_table.shape
    nsub = 2 * A_HEADS
    n_new = q_bd.shape[2] // 2
    group = math.gcd(PAGES_PER_STEP, n_pages)
    n_steps = n_pages // group
    last = n_pages - 1
    page = lambda j: (lambda b, p, pt: (layer, pt[b, jnp.minimum(p * group + j, last)], 0, 0, 0))
    page4 = lambda j: (lambda b, p, pt: (layer, pt[b, jnp.minimum(p * group + j, last)], 0, 0))
    k_specs = [pl.BlockSpec((None, None, nsub, A_HD, PAGE_SIZE), page(j)) for j in range(group)]
    v_specs = [pl.BlockSpec((None, None, PAGE_SIZE * A_HEADS, LANES), page4(j)) for j in range(group)]
    grid_spec = pltpu.PrefetchScalarGridSpec(
        num_scalar_prefetch=1,
        grid=(bn, n_steps + 1),
        in_specs=[pl.BlockSpec((None, A_HEADS, 2 * n_new, LANES), lambda b, p, pt: (b, 0, 0, 0))] + k_specs + v_specs + [
            pl.BlockSpec((None, nsub, A_HD, PAGE_SIZE), lambda b, p, pt: (b, 0, 0, 0)),
            pl.BlockSpec((None, PAGE_SIZE * A_HEADS, LANES), lambda b, p, pt: (b, 0, 0)),
            pl.BlockSpec((None, n_new, BRANCH_W), lambda b, p, pt: (b, 0, 0)),
            pl.BlockSpec((4, A_HD), lambda b, p, pt: (0, 0)),
            pl.BlockSpec((1, LANES), lambda b, p, pt: (0, 0))],
        out_specs=pl.BlockSpec((None, n_new, BRANCH_W), lambda b, p, pt: (b, 0, 0)),
        scratch_shapes=[pltpu.VMEM((nsub * n_new, PAGE_SIZE * group), F32), pltpu.VMEM((nsub * n_new, LANES), F32),
                        pltpu.VMEM((nsub * n_new, LANES), F32), pltpu.VMEM((A_HEADS, 2 * n_new, LANES), F32)],
    )
    return pl.pallas_call(
        functools.partial(_paged_kernel, n_steps=n_steps, group=group, n_new=n_new, lam_init=lam_init),
        grid_spec=grid_spec,
        out_shape=jax.ShapeDtypeStruct((bn, n_new, BRANCH_W), F32),
        compiler_params=_cparams(("parallel", "arbitrary")),
        name="diff_attn_sample",
    )(page_table, q_bd, *([cache_kt] * group), *([cache_v2] * group), knew, vnew, ga, a_lambda, subln)


def _shift_rows(x, prev_row):
    row = lax.broadcasted_iota(jnp.int32, x.shape, 0)
    return jnp.where(row == 0, prev_row, pltpu.roll(x, 1, 0))


def _cumsum_rows(x):
    n = x.shape[0]
    row = lax.broadcasted_iota(jnp.int32, x.shape, 0)
    d = 1
    while d < n:
        x = x + jnp.where(row >= d, pltpu.roll(x, d, 0), 0.0)
        d *= 2
    return x


def _stack2(z):
    first = _first_head(z.shape)
    zero = jnp.zeros_like(z)
    return jnp.concatenate([jnp.where(first, z, zero), jnp.where(first, zero, z)], axis=0)


def _rwkv_kernel(r_ref, k_ref, v_ref, g_ref, lo_ref, sr_ref, sk_ref, sv_ref, sl_ref, st0_ref,
                 mur_ref, muk_ref, muv_ref, mul_ref, w0_ref, a0_ref, kkw_ref, kaw_ref, rk_ref, gnw_ref, gnb_ref,
                 w2_ref, a2_ref, o_ref, st_ref, pr_ref, pk_ref, pv_ref, pl_ref, s_ref,
                 la_ref, rb_ref, v2_ref, bk_ref, x_ref, t_ref, mak_ref, mr_ref, gs_ref, u_ref, *, ln, n_valid):
    c = pl.program_id(1)
    npair = B_HEADS // 2
    l2, l4 = 2 * ln, 4 * ln

    @pl.when(c == 0)
    def _():
        pr_ref[...] = sr_ref[...]
        pk_ref[...] = sk_ref[...]
        pv_ref[...] = sv_ref[...]
        pl_ref[...] = sl_ref[...]
        s_ref[...] = st0_ref[...]

    def mix(x_ref_, prev_ref, mu_ref):
        x = x_ref_[...]
        xm = x + (_shift_rows(x, prev_ref[...]) - x) * mu_ref[...]
        prev_ref[...] = x[n_valid - 1:n_valid]
        return xm

    r = mix(r_ref, pr_ref, mur_ref)
    k = mix(k_ref, pk_ref, muk_ref)
    v = mix(v_ref, pv_ref, muv_ref)
    lo = mix(lo_ref, pl_ref, mul_ref)
    w_pre = w0_ref[...] + _nn(jnp.tanh(lo).astype(BF16), w2_ref[...])
    neg = -w_pre
    softplus = jnp.maximum(neg, 0.0) + jnp.log(1.0 + jnp.exp(-jnp.abs(neg)))
    logw = -jnp.exp(-softplus - 0.5)
    a = _sigmoid(a0_ref[...] + _nn(lo.astype(BF16), a2_ref[...]))
    kk = k * kkw_ref[...]
    k2 = k * (1.0 + (a - 1.0) * kaw_ref[...])
    if n_valid < ln:
        live = lax.broadcasted_iota(jnp.int32, logw.shape, 0) < n_valid
        logw = jnp.where(live, logw, 0.0)
        kk = jnp.where(live, kk, 0.0)
        k2 = jnp.where(live, k2, 0.0)
    cum = _cumsum_rows(logw)
    e_in = jnp.exp(cum)
    e_ex = jnp.exp(cum - logw)
    e_inv = jnp.exp(-cum)
    c_last = cum[ln - 1:ln]
    e_last = jnp.exp(c_last - cum)
    decay_l = jnp.exp(c_last)

    for pr in range(npair):
        cols = slice(LANES * pr, LANES * (pr + 1))
        kkp = kk[:, cols]
        kkp = kkp * lax.rsqrt(_seg_sum(kkp * kkp) + 1e-12)
        b = kkp * a[:, cols]
        k2p = k2[:, cols]
        la_ref[pr] = jnp.concatenate([_stack2(-kkp * e_ex[:, cols]), _stack2(r[:, cols] * e_in[:, cols])], 0).astype(BF16)
        rb_ref[pr] = jnp.concatenate([_stack2(b * e_inv[:, cols]), _stack2(k2p * e_inv[:, cols])], 0).astype(BF16)
        bk_ref[pr] = jnp.concatenate([_stack2(b * e_last[:, cols]), _stack2(k2p * e_last[:, cols])], 0).astype(BF16)
        v2_ref[pr] = _stack2(v[:, cols]).astype(BF16)

    rr = lax.broadcasted_iota(jnp.int32, (l2, l4), 0) & (ln - 1)
    cc_full = lax.broadcasted_iota(jnp.int32, (l2, l4), 1)
    cc = cc_full & (ln - 1)
    strict_right = (cc < rr) & (cc_full >= l2)
    incl = cc <= rr
    rs = lax.broadcasted_iota(jnp.int32, (l2, l2), 0)
    cs = lax.broadcasted_iota(jnp.int32, (l2, l2), 1)
    strict_sq = (cs & (ln - 1)) < (rs & (ln - 1))
    eye = (rs == cs).astype(F32)
    for pr in range(npair):
        la = la_ref[pr]
        mm = _nt(la, rb_ref[pr])
        xa = mm[0:l2]
        x = jnp.where(strict_sq, xa[:, 0:l2], 0.0)
        x_ref[pr] = x
        t_ref[pr] = eye + x
        mak_ref[pr] = jnp.where(strict_right, xa, 0.0).astype(BF16)
        mr_ref[pr] = jnp.where(incl, mm[l2:l4], 0.0).astype(BF16)
        gs_ref[pr] = _nt(la, s_ref[pr].astype(BF16))
    for pr in range(npair):
        v2 = v2_ref[pr]
        u_ref[pr] = gs_ref[pr, 0:l2, :] + _nn(mak_ref[pr], jnp.concatenate([v2, v2], axis=0))

    n = 1
    while 2 * n < ln:
        for pr in range(npair):
            xb = x_ref[pr].astype(BF16)
            x_ref[pr] = _nn(xb, xb)
        for pr in range(npair):
            t = t_ref[pr]
            t_ref[pr] = t + _nn(t.astype(BF16), x_ref[pr].astype(BF16))
        n *= 2

    for pr in range(npair):
        u_ref[pr] = _nn(t_ref[pr].astype(BF16), u_ref[pr].astype(BF16))

    rk_all = r * k2 * rk_ref[...]
    gate = _silu(g_ref[...])
    for pr in range(npair):
        cols = slice(LANES * pr, LANES * (pr + 1))
        u2 = u_ref[pr]
        v2 = v2_ref[pr]
        y2 = gs_ref[pr, l2:l4, :] + _nn(mr_ref[pr], jnp.concatenate([u2.astype(BF16), v2], axis=0))
        y = y2[0:ln] + y2[ln:l2]
        uvt = jnp.concatenate([u2, v2.astype(F32)], axis=0).T.astype(BF16)
        s_ref[pr] = s_ref[pr] * decay_l[:, cols] + _nn(uvt, bk_ref[pr])
        mu = _seg_sum(y) * (1.0 / HALF)
        dy = y - mu
        var = _seg_sum(dy * dy) * (1.0 / HALF)
        yn = dy * lax.rsqrt(var + B_GN_EPS) * gnw_ref[:, cols] + gnb_ref[:, cols]
        yn = yn + _seg_sum(rk_all[:, cols]) * v[:, cols]
        o_ref[:, cols] = (yn * gate[:, cols]).astype(BF16)

    @pl.when(c == pl.num_programs(1) - 1)
    def _():
        st_ref[...] = s_ref[...]


def _rwkv_call(p, lora, shift_r, shift_k, shift_v, shift_l, state_bd, prm, bn, s, ln, n_valid):
    nc = s // ln
    npair = B_HEADS // 2
    rb, kb, vb, gb = (c // BRANCH_W for c in (COL_BR, COL_BK, COL_BV, COL_BG))
    tok = lambda cb: pl.BlockSpec((ln, BRANCH_W), lambda b, c: (b * nc + c, cb))
    per_b = lambda w: pl.BlockSpec((None, 1, w), lambda b, c: (b, 0, 0))
    row = lambda w: pl.BlockSpec((1, w), lambda b, c: (0, 0))
    st_spec = pl.BlockSpec((None, npair, LANES, LANES), lambda b, c: (b, 0, 0, 0))
    l2, l4 = 2 * ln, 4 * ln
    return pl.pallas_call(
        functools.partial(_rwkv_kernel, ln=ln, n_valid=n_valid),
        grid=(bn, nc),
        in_specs=[tok(rb), tok(kb), tok(vb), tok(gb),
                  pl.BlockSpec((ln, LANES), lambda b, c: (b * nc + c, 0)),
                  per_b(BRANCH_W), per_b(BRANCH_W), per_b(BRANCH_W), per_b(LANES), st_spec,
                  row(BRANCH_W), row(BRANCH_W), row(BRANCH_W), row(LANES),
                  row(BRANCH_W), row(BRANCH_W), row(BRANCH_W), row(BRANCH_W), row(BRANCH_W), row(BRANCH_W),
                  row(BRANCH_W),
                  pl.BlockSpec((LANES, BRANCH_W), lambda b, c: (0, 0)),
                  pl.BlockSpec((LANES, BRANCH_W), lambda b, c: (0, 0))],
        out_specs=[pl.BlockSpec((ln, BRANCH_W), lambda b, c: (b * nc + c, 0)), st_spec],
        out_shape=[jax.ShapeDtypeStruct((bn * s, BRANCH_W), BF16),
                   jax.ShapeDtypeStruct((bn, npair, LANES, LANES), F32)],
        scratch_shapes=[pltpu.VMEM((1, BRANCH_W), F32), pltpu.VMEM((1, BRANCH_W), F32),
                        pltpu.VMEM((1, BRANCH_W), F32), pltpu.VMEM((1, LANES), F32),
                        pltpu.VMEM((npair, LANES, LANES), F32),
                        pltpu.VMEM((npair, l4, LANES), BF16), pltpu.VMEM((npair, l4, LANES), BF16),
                        pltpu.VMEM((npair, l2, LANES), BF16), pltpu.VMEM((npair, l4, LANES), BF16),
                        pltpu.VMEM((npair, l2, l2), F32), pltpu.VMEM((npair, l2, l2), F32),
                        pltpu.VMEM((npair, l2, l4), BF16), pltpu.VMEM((npair, l2, l4), BF16),
                        pltpu.VMEM((npair, l4, LANES), F32), pltpu.VMEM((npair, l2, LANES), F32)],
        compiler_params=_cparams(("parallel", "arbitrary")),
        name="rwkv7_chunked",
    )(p, p, p, p, lora, shift_r, shift_k, shift_v, shift_l, state_bd,
      prm["mu_r"], prm["mu_k"], prm["mu_v"], prm["mu_l"], prm["w0"], prm["a0"], prm["kk"], prm["ka"], prm["rk"],
      prm["gn_w"], prm["gn_b"], prm["w2"], prm["a2"])


def _ret_kernel(q_ref, k_ref, v_ref, g_ref, cos_ref, sin_ref, st0_ref, o_ref, st_ref, s_ref, *, cn, n_valid):
    c = pl.program_id(1)

    @pl.when(c == 0)
    def _():
        s_ref[...] = st0_ref[...]

    cos = cos_ref[...]
    sin = sin_ref[...]
    rowc = lax.broadcasted_iota(jnp.int32, (cn, cn), 0)
    colc = lax.broadcasted_iota(jnp.int32, (cn, cn), 1)
    dist = (rowc - colc).astype(F32)
    causal = colc <= rowc
    n_idx = lax.broadcasted_iota(jnp.int32, (cn, 1), 0).astype(F32)
    first128 = _first_head((1, LANES))
    first256 = lax.broadcasted_iota(jnp.int32, (1, 2 * LANES), 1) < LANES
    ri = lax.broadcasted_iota(jnp.int32, (LANES, 2 * LANES), 0) < HALF
    ci = lax.broadcasted_iota(jnp.int32, (LANES, 2 * LANES), 1) < LANES
    live = lax.broadcasted_iota(jnp.int32, (cn, 1), 0) < n_valid
    for pr in range(C_HEADS // 2):
        lg = [math.log(1.0 - 2.0 ** (-5.0 - (2 * pr + j))) for j in range(2)]
        qc = slice(LANES * pr, LANES * (pr + 1))
        vc = slice(2 * LANES * pr, 2 * LANES * (pr + 1))
        q = _rope(q_ref[:, qc], cos, sin)
        k = _rope(k_ref[:, qc], cos, sin) * (C_KD ** -0.5)
        if n_valid < cn:
            k = jnp.where(live, k, 0.0)
        v = v_ref[:, vc]
        vb = v.astype(BF16)
        kb = k.astype(BF16)
        zero = jnp.zeros_like(q)
        outs = []
        for j in range(2):
            qj = jnp.where(first128, q, zero) if j == 0 else jnp.where(first128, zero, q)
            sc = _nt(qj.astype(BF16), kb)
            sc = sc * jnp.where(causal, jnp.exp(jnp.maximum(dist, 0.0) * lg[j]), 0.0)
            outs.append(_nn(sc.astype(BF16), vb[:, LANES * j:LANES * (j + 1)]))
        st = s_ref[pr]
        xi = jnp.where(first256, jnp.exp((n_idx + 1.0) * lg[0]), jnp.exp((n_idx + 1.0) * lg[1]))
        o = jnp.concatenate(outs, axis=1) + _nn(q.astype(BF16), st.astype(BF16)) * xi
        zeta = jnp.where(first128, jnp.exp((n_valid - 1.0 - n_idx) * lg[0]), jnp.exp((n_valid - 1.0 - n_idx) * lg[1]))
        upd = _nn((k * zeta).T.astype(BF16), vb)
        gch = jnp.where(first256, math.exp(n_valid * lg[0]), math.exp(n_valid * lg[1]))
        s_ref[pr] = st * gch + jnp.where(ri == ci, upd, 0.0)
        gate = _silu(g_ref[:, vc])
        for j in range(2):
            oj = o[:, LANES * j:LANES * (j + 1)]
            oj = oj * lax.rsqrt(jnp.mean(oj * oj, axis=-1, keepdims=True) + NORM_EPS)
            o_ref[:, 2 * LANES * pr + LANES * j:2 * LANES * pr + LANES * (j + 1)] = (
                oj * gate[:, LANES * j:LANES * (j + 1)]).astype(BF16)

    @pl.when(c == pl.num_programs(1) - 1)
    def _():
        st_ref[...] = s_ref[...]


def _ret_call(p, cos, sin, state_bd, bn, s, cn, n_valid):
    nc = s // cn
    npair = C_HEADS // 2
    ntab = cos.shape[0] // cn
    st_spec = pl.BlockSpec((None, npair, LANES, 2 * LANES), lambda b, c: (b, 0, 0, 0))
    return pl.pallas_call(
        functools.partial(_ret_kernel, cn=cn, n_valid=n_valid),
        grid=(bn, nc),
        in_specs=[pl.BlockSpec((cn, 512), lambda b, c: (b * nc + c, COL_CQ // 512)),
                  pl.BlockSpec((cn, 512), lambda b, c: (b * nc + c, COL_CK // 512)),
                  pl.BlockSpec((cn, BRANCH_W), lambda b, c: (b * nc + c, COL_CV // BRANCH_W)),
                  pl.BlockSpec((cn, BRANCH_W), lambda b, c: (b * nc + c, COL_CG // BRANCH_W)),
                  pl.BlockSpec((cn, LANES), lambda b, c: (c % ntab, 0)),
                  pl.BlockSpec((cn, LANES), lambda b, c: (c % ntab, 0)),
                  st_spec],
        out_specs=[pl.BlockSpec((cn, BRANCH_W), lambda b, c: (b * nc + c, 0)), st_spec],
        out_shape=[jax.ShapeDtypeStruct((bn * s, BRANCH_W), BF16),
                   jax.ShapeDtypeStruct((bn, npair, LANES, 2 * LANES), F32)],
        scratch_shapes=[pltpu.VMEM((npair, LANES, 2 * LANES), F32)],
        compiler_params=_cparams(("parallel", "arbitrary")),
        name="retention_chunkwise",
    )(p, p, p, p, cos, sin, state_bd)


def _merge_kernel(oa_ref, ob_ref, oc_ref, wb_ref, g0_ref, g1_ref, g2_ref, o_ref):
    acc = _sigmoid(g0_ref[...]) * _nn(oa_ref[...], wb_ref[0])
    acc = acc + _sigmoid(g1_ref[...]) * _nn(ob_ref[...], wb_ref[1])
    acc = acc + _sigmoid(g2_ref[...]) * _nn(oc_ref[...], wb_ref[2])
    o_ref[...] = acc.astype(BF16)


def _merge_call(oa, ob, oc, wb, layer, p, tm, tn):
    m = oa.shape[0]
    nj = D_MODEL // tn
    assert COL_GATE % tn == 0 and D_MODEL % tn == 0
    act = pl.BlockSpec((tm, BRANCH_W), lambda i, j: (i, 0))
    gate = lambda n: pl.BlockSpec((tm, tn), lambda i, j: (i, (COL_GATE + n * D_MODEL) // tn + j))
    return pl.pallas_call(
        _merge_kernel,
        grid=(m // tm, nj),
        in_specs=[act, act, act, pl.BlockSpec((None, 3, BRANCH_W, tn), lambda i, j: (layer, 0, 0, j)), gate(0), gate(1), gate(2)],
        out_specs=pl.BlockSpec((tm, tn), lambda i, j: (i, j)),
        out_shape=jax.ShapeDtypeStruct((m, D_MODEL), BF16),
        compiler_params=_cparams(("parallel", "arbitrary")),
        name="branch_merge",
    )(oa, ob, oc, wb, p, p, p)


def _out_kernel(a_ref, b_ref, x_ref, o_ref):
    o_ref[...] = x_ref[...] + _nn(a_ref[...], b_ref[...])


def _out_call(merged, w_out, layer, x2d, tm, tn):
    m = merged.shape[0]
    return pl.pallas_call(
        _out_kernel,
        grid=(m // tm, D_MODEL // tn),
        in_specs=[pl.BlockSpec((tm, D_MODEL), lambda i, j: (i, 0)), pl.BlockSpec((None, D_MODEL, tn), lambda i, j: (layer, 0, j)),
                  pl.BlockSpec((tm, tn), lambda i, j: (i, j))],
        out_specs=pl.BlockSpec((tm, tn), lambda i, j: (i, j)),
        out_shape=jax.ShapeDtypeStruct((m, D_MODEL), F32),
        compiler_params=_cparams(("parallel", "arbitrary")),
        name="out_proj_residual",
    )(merged, w_out, x2d)


def _rope_tables(pos, inv_freq):
    ang = pos.astype(F32)[:, None] * inv_freq[None, :]
    cos = jnp.cos(ang)
    sin = jnp.sin(ang)
    cos = jnp.concatenate([cos, cos, cos, cos], axis=1)
    sin = jnp.concatenate([-sin, sin, -sin, sin], axis=1)
    return cos, sin


def _permute_w_in(w):
    b0 = 4096
    c0 = b0 + B_SHIFT + BRANCH_W
    cut = lambda lo, hi: w[..., lo:hi]
    main = jnp.concatenate([cut(0, b0 + 1024), cut(b0 + 1088, b0 + 3136), cut(b0 + 3200, b0 + 4224),
                            cut(c0, c0 + 3072 + 3 * D_MODEL)], axis=-1)
    lora = jnp.concatenate([cut(b0 + 1024, b0 + 1088), cut(b0 + 3136, b0 + 3200)], axis=-1)
    return main.astype(BF16), lora.astype(BF16)


def _split_shift(sh):
    return (sh[..., 0:1024], sh[..., 1088:2112], sh[..., 2112:3136],
            jnp.concatenate([sh[..., 1024:1088], sh[..., 3136:3200]], axis=-1))


def _join_shift(r, k, v, lo):
    return jnp.concatenate([r, lo[..., 0:64], k, v, lo[..., 64:128]], axis=-1)


def _pairs_to_blockdiag(st):
    bn, nh, r, c = st.shape
    st = st.reshape(bn, nh // 2, 2, r, c)
    z = jnp.zeros_like(st[:, :, 0])
    top = jnp.concatenate([st[:, :, 0], z], axis=-1)
    bot = jnp.concatenate([z, st[:, :, 1]], axis=-1)
    return jnp.concatenate([top, bot], axis=-2)


def _blockdiag_to_pairs(bd):
    bn, npair, r2, c2 = bd.shape
    r, c = r2 // 2, c2 // 2
    return jnp.stack([bd[:, :, :r, :c], bd[:, :, r:, c:]], axis=2).reshape(bn, 2 * npair, r, c)


def _layer_params(l, w_main, w_lora, w_branch_bf, w_out_bf, norm_w, a_qnorm, a_knorm, a_lambda, a_subln, b_mu, b_w0,
                  b_w2, b_a0, b_a2, b_kk, b_ka, b_rk, b_gn_w, b_gn_b):
    mu_r, mu_k, mu_v, mu_l = _split_shift(b_mu[l])
    zeros = jnp.zeros((B_LORA, BRANCH_W), F32)
    row = lambda t: t.reshape(1, -1)
    return dict(
        layer=l, w_main=w_main, w_lora=w_lora, w_branch=w_branch_bf, w_out=w_out_bf, norm_w=row(norm_w[l]),
        qn=row(jnp.tile(a_qnorm[l], 2)), kn=row(jnp.tile(a_knorm[l], 2)), a_lambda=a_lambda[l], subln=row(a_subln[l]),
        lam_init=0.8 - 0.6 * math.exp(-0.3 * l),
        rwkv=dict(mu_r=row(mu_r), mu_k=row(mu_k), mu_v=row(mu_v), mu_l=row(mu_l), w0=row(b_w0[l]), a0=row(b_a0[l]),
                  kk=row(b_kk[l]), ka=row(b_ka[l]), rk=row(b_rk[l]), gn_w=row(b_gn_w[l]), gn_b=row(b_gn_b[l]),
                  w2=jnp.concatenate([b_w2[l], zeros], axis=0).astype(BF16),
                  a2=jnp.concatenate([zeros, b_a2[l]], axis=0).astype(BF16)))


def _tiles(m):
    return min(m, 1024)


def _trunk_layer(x2d, bn, s, n_valid, lp, tabs_a, tabs_c, attend, shift0, rwkv0_bd, ret0_bd, chunk_b, chunk_c):
    m = bn * s
    tm = _tiles(m)
    h = _rms_call(x2d, lp["norm_w"], min(m, 512))
    small = m <= LANES
    p = _mm_call(h, lp["w_main"], lp["layer"], tm, N_MAIN // 4 if small else 1024, "in_proj")
    lora = _mm_call(h, lp["w_lora"], lp["layer"], tm, LANES, "in_proj_lora")
    t_qk = min(tabs_a[0].shape[0], 512)
    q_bf, k_f32, k_bf = _qk_call(p, lp["qn"], lp["kn"], tabs_a[0], tabs_a[1], t_qk)
    oa = attend(q_bf, k_f32, k_bf, p)
    sr, sk, sv, sl = shift0
    ob, rwkv_bd = _rwkv_call(p, lora, sr, sk, sv, sl, rwkv0_bd, lp["rwkv"], bn, s, chunk_b, min(n_valid, chunk_b))
    oc, ret_bd = _ret_call(p, tabs_c[0], tabs_c[1], ret0_bd, bn, s, chunk_c, min(n_valid, chunk_c))
    merged = _merge_call(oa, ob, oc, lp["w_branch"], lp["layer"], p, min(m, 512), 1024)
    y = _out_call(merged, lp["w_out"], lp["layer"], x2d, tm, D_MODEL if small else 1024)
    return y, k_f32, p, lora, rwkv_bd, ret_bd


def kernel(x_prompt, x_sample, cache_k, cache_v, page_table, state_rwkv, state_shift, state_ret, norm_w, w_in, a_qnorm, a_knorm, a_lambda, a_subln, b_mu, b_w0, b_w2, b_a0, b_a2, b_kk, b_ka, b_rk, b_gn_w, b_gn_b, w_branch, w_out):
    bp, sp, _ = x_prompt.shape
    bs, ss, _ = x_sample.shape
    ss_pad = 16
    depth = w_in.shape[0]
    n_pool = cache_k.shape[1]

    w_main, w_lora = _permute_w_in(w_in)
    w_branch_bf = w_branch.astype(BF16)
    w_out_bf = w_out.astype(BF16)
    cache_kt = jnp.transpose(cache_k, (0, 1, 3, 4, 2))
    cache_v2 = cache_v.reshape(depth, n_pool, PAGE_SIZE * A_HEADS, LANES)

    inv_a = ROPE_THETA ** (-jnp.arange(A_HD // 2, dtype=F32) / (A_HD // 2))
    inv_c = ROPE_THETA ** (-jnp.linspace(0.0, 1.0, C_KD // 2, dtype=F32))
    pos_p = jnp.arange(sp)
    pos_s = PAST_LEN + jnp.arange(ss_pad)
    tabs_a_p, tabs_c_p = _rope_tables(pos_p, inv_a), _rope_tables(pos_p, inv_c)
    tabs_a_s = tuple(jnp.tile(t, (bs, 1)) for t in _rope_tables(pos_s, inv_a))
    tabs_c_s = _rope_tables(pos_s, inv_c)

    xp = x_prompt.reshape(bp * sp, D_MODEL)
    xs = jnp.pad(x_sample, ((0, 0), (0, ss_pad - ss), (0, 0))).reshape(bs * ss_pad, D_MODEL)
    zero_shift = (jnp.zeros((bp, 1, BRANCH_W), F32),) * 3 + (jnp.zeros((bp, 1, LANES), F32),)
    rwkv0_p = jnp.zeros((bp, B_HEADS // 2, LANES, LANES), F32)
    ret0_p = jnp.zeros((bp, C_HEADS // 2, LANES, 2 * LANES), F32)

    outs = [[] for _ in range(10)]
    for l in range(depth):
        lp = _layer_params(l, w_main, w_lora, w_branch_bf, w_out_bf, norm_w, a_qnorm, a_knorm, a_lambda, a_subln, b_mu,
                           b_w0, b_w2, b_a0, b_a2, b_kk, b_ka, b_rk, b_gn_w, b_gn_b)

        def attend_p(q_bf, k_f32, k_bf, p, lp=lp):
            return _flash_call(q_bf, k_bf, p, lp["a_lambda"], lp["subln"], bp, sp, min(sp, 512), lp["lam_init"])

        xp, k_new, p, lora, rw, rt = _trunk_layer(xp, bp, sp, sp, lp, tabs_a_p, tabs_c_p, attend_p, zero_shift,
                                                  rwkv0_p, ret0_p, 64, 128)
        last = p.reshape(bp, sp, N_MAIN)[:, sp - 1]
        outs[0].append(k_new.reshape(bp, sp, 2 * A_HEADS, A_HD))
        outs[1].append(p.reshape(bp, sp, N_MAIN)[:, :, COL_AV:COL_AV + BRANCH_W].reshape(bp, sp, A_HEADS, LANES))
        outs[2].append(_blockdiag_to_pairs(rw))
        outs[3].append(_join_shift(last[:, COL_BR:COL_BR + 1024], last[:, COL_BK:COL_BK + 1024],
                                   last[:, COL_BV:COL_BV + 1024], lora.reshape(bp, sp, LANES)[:, sp - 1]))
        outs[4].append(_blockdiag_to_pairs(rt))

        def attend_s(q_bf, k_f32, k_bf, p, lp=lp, l=l):
            q5 = q_bf.reshape(bs, ss_pad, A_HEADS, 2, A_HD)[:, :ss].transpose(0, 2, 3, 1, 4)
            zq = jnp.zeros_like(q5[:, :, 0])
            q_bd = jnp.concatenate([jnp.concatenate([q5[:, :, 0], zq], axis=-1),
                                    jnp.concatenate([zq, q5[:, :, 1]], axis=-1)], axis=2)
            kn = k_f32.reshape(bs, ss_pad, 2 * A_HEADS, A_HD)[:, :ss].transpose(0, 2, 3, 1)
            kn = jnp.pad(kn, ((0, 0), (0, 0), (0, 0), (0, PAGE_SIZE - ss)))
            p3 = p.reshape(bs, ss_pad, N_MAIN)[:, :ss]
            vn = p3[:, :, COL_AV:COL_AV + BRANCH_W].reshape(bs, ss, A_HEADS, LANES)
            vn = jnp.pad(vn, ((0, 0), (0, PAGE_SIZE - ss), (0, 0), (0, 0))).reshape(bs, PAGE_SIZE * A_HEADS, LANES)
            ga = p3[:, :, COL_AG:COL_AG + BRANCH_W]
            o = _paged_call(page_table, q_bd, cache_kt, cache_v2, kn, vn, ga, lp["a_lambda"], lp["subln"], l,
                            lp["lam_init"])
            return jnp.pad(o, ((0, 0), (0, ss_pad - ss), (0, 0))).astype(BF16).reshape(bs * ss_pad, BRANCH_W)

        sh = tuple(t[:, None, :] for t in _split_shift(state_shift[l]))
        rwkv0_s = _pairs_to_blockdiag(state_rwkv[l])
        ret0_s = _pairs_to_blockdiag(state_ret[l])
        xs, k_new, p, lora, rw, rt = _trunk_layer(xs, bs, ss_pad, ss, lp, tabs_a_s, tabs_c_s, attend_s, sh,
                                                  rwkv0_s, ret0_s, ss_pad, ss_pad)
        p3 = p.reshape(bs, ss_pad, N_MAIN)
        last = p3[:, ss - 1]
        outs[5].append(k_new.reshape(bs, ss_pad, 2 * A_HEADS, A_HD)[:, :ss])
        outs[6].append(p3[:, :ss, COL_AV:COL_AV + BRANCH_W].reshape(bs, ss, A_HEADS, LANES))
        outs[7].append(_blockdiag_to_pairs(rw))
        outs[8].append(_join_shift(last[:, COL_BR:COL_BR + 1024], last[:, COL_BK:COL_BK + 1024],
                                   last[:, COL_BV:COL_BV + 1024], lora.reshape(bs, ss_pad, LANES)[:, ss - 1]))
        outs[9].append(_blockdiag_to_pairs(rt))

    st = [jnp.stack(o) for o in outs]
    y_prompt = xp.reshape(bp, sp, D_MODEL)
    y_sample = xs.reshape(bs, ss_pad, D_MODEL)[:, :ss]
    return (y_prompt, y_sample, st[0], st[1], st[2], st[3], st[4], st[5], st[6], st[7], st[8], st[9])
```

```python
import functools
import math

import jax
import jax.numpy as jnp
from jax import lax
from jax.experimental import pallas as pl
from jax.experimental.pallas import tpu as pltpu

F32 = jnp.float32
BF16 = jnp.bfloat16

D_MODEL = 2048
DEPTH = 4
PAST_LEN = 16384
PAGE_SIZE = 128
BRANCH_W = 1024
A_HEADS = 8
A_HD = 64
B_HEADS = 16
B_LORA = 64
B_SHIFT = 3 * BRANCH_W + 2 * B_LORA
B_GN_EPS = 64e-5
C_HEADS = 8
C_KD = 64
ROPE_THETA = 10000.0
NORM_EPS = 1e-6
LANES = 128
HALF = 64
NEG_BIG = -1e30

COL_AQ, COL_AK, COL_AV, COL_AG = 0, 1024, 2048, 3072
COL_BR, COL_BK, COL_BV, COL_BG = 4096, 5120, 6144, 7168
COL_CQ, COL_CK, COL_CV, COL_CG = 8192, 8704, 9216, 10240
COL_GATE = 11264
N_MAIN = 17408
VMEM_LIMIT = 56 * 1024 * 1024


def _cparams(sem):
    return pltpu.CompilerParams(dimension_semantics=sem, vmem_limit_bytes=VMEM_LIMIT)


def _nt(a, b):
    return lax.dot_general(a, b, (((1,), (1,)), ((), ())), preferred_element_type=F32)


def _nn(a, b):
    return jnp.dot(a, b, preferred_element_type=F32)


def _silu(x):
    return x * (1.0 / (1.0 + jnp.exp(-x)))


def _sigmoid(x):
    return 1.0 / (1.0 + jnp.exp(-x))


def _tile_lanes(x, n):
    return x if n == 1 else jnp.concatenate([x] * n, axis=1)


def _first_head(shape):
    return lax.broadcasted_iota(jnp.int32, shape, len(shape) - 1) < HALF


def _seg_sum(x):
    first = _first_head(x.shape)
    zero = jnp.zeros_like(x)
    sa = jnp.sum(jnp.where(first, x, zero), axis=-1, keepdims=True)
    sb = jnp.sum(jnp.where(first, zero, x), axis=-1, keepdims=True)
    return jnp.where(first, sa, sb)


def _rope(x, cos, sin_signed):
    lane = lax.broadcasted_iota(jnp.int32, x.shape, 1)
    first_half = (lane % HALF) < (HALF // 2)
    partner = jnp.where(first_half, pltpu.roll(x, LANES - HALF // 2, 1), pltpu.roll(x, HALF // 2, 1))
    return x * cos + partner * sin_signed


def _rms_kernel(x_ref, w_ref, o_ref):
    x = x_ref[...]
    y = x * lax.rsqrt(jnp.mean(x * x, axis=-1, keepdims=True) + NORM_EPS)
    o_ref[...] = (y * w_ref[...]).astype(BF16)


def _rms_call(x2d, w, tm):
    m = x2d.shape[0]
    return pl.pallas_call(
        _rms_kernel,
        grid=(m // tm,),
        in_specs=[pl.BlockSpec((tm, D_MODEL), lambda i: (i, 0)), pl.BlockSpec((1, D_MODEL), lambda i: (0, 0))],
        out_specs=pl.BlockSpec((tm, D_MODEL), lambda i: (i, 0)),
        out_shape=jax.ShapeDtypeStruct((m, D_MODEL), BF16),
        compiler_params=_cparams(("parallel",)),
        name="rmsnorm",
    )(x2d, w)


def _mm_kernel(a_ref, b_ref, o_ref):
    o_ref[...] = _nn(a_ref[...], b_ref[...])


def _mm_call(a, b, layer, tm, tn, name):
    m, k = a.shape
    n = b.shape[2]
    return pl.pallas_call(
        _mm_kernel,
        grid=(m // tm, n // tn),
        in_specs=[pl.BlockSpec((tm, k), lambda i, j: (i, 0)), pl.BlockSpec((None, k, tn), lambda i, j: (layer, 0, j))],
        out_specs=pl.BlockSpec((tm, tn), lambda i, j: (i, j)),
        out_shape=jax.ShapeDtypeStruct((m, n), F32),
        compiler_params=_cparams(("parallel", "arbitrary")),
        name=name,
    )(a, b)


def _qk_kernel(p_ref, qn_ref, kn_ref, cos_ref, sin_ref, q_ref, k32_ref, kbf_ref):
    cos = cos_ref[...]
    sin = sin_ref[...]

    def norm_rope(x, nw):
        ms = _seg_sum(x * x) * (1.0 / HALF)
        return _rope(x * lax.rsqrt(ms + NORM_EPS) * nw, cos, sin)

    for c in range(BRANCH_W // LANES):
        cols = slice(LANES * c, LANES * (c + 1))
        q_ref[:, cols] = (norm_rope(p_ref[:, cols], qn_ref[...]) * (A_HD ** -0.5)).astype(BF16)
        kc = norm_rope(p_ref[:, BRANCH_W + LANES * c:BRANCH_W + LANES * (c + 1)], kn_ref[...])
        k32_ref[:, cols] = kc
        kbf_ref[:, cols] = kc.astype(BF16)


def _qk_call(p, qn, kn, cos, sin, tm):
    m = p.shape[0]
    ntab = cos.shape[0] // tm
    blk = pl.BlockSpec((tm, BRANCH_W), lambda i: (i, 0))
    return pl.pallas_call(
        _qk_kernel,
        grid=(m // tm,),
        in_specs=[pl.BlockSpec((tm, 2 * BRANCH_W), lambda i: (i, 0)),
                  pl.BlockSpec((1, LANES), lambda i: (0, 0)), pl.BlockSpec((1, LANES), lambda i: (0, 0)),
                  pl.BlockSpec((tm, LANES), lambda i: (i % ntab, 0)), pl.BlockSpec((tm, LANES), lambda i: (i % ntab, 0))],
        out_specs=[blk, blk, blk],
        out_shape=[jax.ShapeDtypeStruct((m, BRANCH_W), BF16), jax.ShapeDtypeStruct((m, BRANCH_W), F32),
                   jax.ShapeDtypeStruct((m, BRANCH_W), BF16)],
        compiler_params=_cparams(("parallel",)),
        name="qk_norm_rope",
    )(p, qn, kn, cos, sin)


def _lambda_value(lam_ref, lam_init):
    lv = lam_ref[...]
    s1 = jnp.sum(lv[0:1] * lv[1:2], axis=-1, keepdims=True)
    s2 = jnp.sum(lv[2:3] * lv[3:4], axis=-1, keepdims=True)
    return jnp.exp(s1) - jnp.exp(s2) + lam_init


def _subln_gate(o1, o2, lam, sub, ga, lam_init):
    a = o1 - lam * o2
    y = a * lax.rsqrt(jnp.mean(a * a, axis=-1, keepdims=True) + NORM_EPS) * sub
    return (y * (1.0 - lam_init)) * _silu(ga)


def _flash_kernel(qi_ref, ki_ref, q_ref, k_ref, v_ref, ga_ref, lam_ref, sub_ref, o_ref, q2_ref, m_ref, l_ref, acc_ref, *,
                  tq, lam_init):
    step = pl.program_id(2)
    qi = qi_ref[step]
    ki = ki_ref[step]

    @pl.when(ki == 0)
    def _():
        q = q_ref[...]
        first = _first_head(q.shape)
        zero = jnp.zeros_like(q)
        q2_ref[0:tq, :] = jnp.where(first, q, zero)
        q2_ref[tq:2 * tq, :] = jnp.where(first, zero, q)
        m_ref[...] = jnp.full(m_ref.shape, NEG_BIG, F32)
        l_ref[...] = jnp.zeros(l_ref.shape, F32)
        acc_ref[...] = jnp.zeros(acc_ref.shape, F32)

    def accumulate(masked):
        s = _nt(q2_ref[...], k_ref[...])
        if masked:
            row = lax.broadcasted_iota(jnp.int32, s.shape, 0)
            col = lax.broadcasted_iota(jnp.int32, s.shape, 1)
            s = jnp.where(col <= jnp.where(row >= tq, row - tq, row), s, NEG_BIG)
        m_prev = m_ref[...]
        m_new = jnp.maximum(m_prev, jnp.max(s, axis=-1, keepdims=True))
        alpha = jnp.exp(m_prev - m_new)
        p = jnp.exp(s - _tile_lanes(m_new, s.shape[1] // LANES))
        l_ref[...] = alpha * l_ref[...] + jnp.sum(p, axis=-1, keepdims=True)
        acc_ref[...] = alpha * acc_ref[...] + _nn(p.astype(BF16), v_ref[...].astype(BF16))
        m_ref[...] = m_new

    @pl.when(ki < qi)
    def _():
        accumulate(False)

    @pl.when(ki == qi)
    def _():
        accumulate(True)
        o = acc_ref[...] / l_ref[...]
        lam = _lambda_value(lam_ref, lam_init)
        y = _subln_gate(o[0:tq], o[tq:2 * tq], lam, sub_ref[...], ga_ref[...], lam_init)
        o_ref[...] = y.astype(BF16)


def _flash_call(q, k, p, a_lambda, subln, bn, s, tq, lam_init):
    nq = s // tq
    vb, gb = COL_AV // LANES, COL_AG // LANES
    pairs = [(i, j) for i in range(nq) for j in range(i + 1)]
    qi_tab = jnp.asarray([i for i, _ in pairs], jnp.int32)
    ki_tab = jnp.asarray([j for _, j in pairs], jnp.int32)
    grid_spec = pltpu.PrefetchScalarGridSpec(
        num_scalar_prefetch=2,
        grid=(bn, A_HEADS, len(pairs)),
        in_specs=[pl.BlockSpec((tq, LANES), lambda b, h, t, qi, ki: (b * nq + qi[t], h)),
                  pl.BlockSpec((tq, LANES), lambda b, h, t, qi, ki: (b * nq + ki[t], h)),
                  pl.BlockSpec((tq, LANES), lambda b, h, t, qi, ki: (b * nq + ki[t], vb + h)),
                  pl.BlockSpec((tq, LANES), lambda b, h, t, qi, ki: (b * nq + qi[t], gb + h)),
                  pl.BlockSpec((4, A_HD), lambda b, h, t, qi, ki: (0, 0)),
                  pl.BlockSpec((1, LANES), lambda b, h, t, qi, ki: (0, 0))],
        out_specs=pl.BlockSpec((tq, LANES), lambda b, h, t, qi, ki: (b * nq + qi[t], h)),
        scratch_shapes=[pltpu.VMEM((2 * tq, LANES), BF16), pltpu.VMEM((2 * tq, LANES), F32),
                        pltpu.VMEM((2 * tq, LANES), F32), pltpu.VMEM((2 * tq, LANES), F32)],
    )
    return pl.pallas_call(
        functools.partial(_flash_kernel, tq=tq, lam_init=lam_init),
        grid_spec=grid_spec,
        out_shape=jax.ShapeDtypeStruct((bn * s, BRANCH_W), BF16),
        compiler_params=_cparams(("parallel", "parallel", "arbitrary")),
        name="diff_attn_prompt",
    )(qi_tab, ki_tab, q, k, p, p, a_lambda, subln)


PAGES_PER_STEP = 16


def _paged_kernel(pt_ref, q_ref, *refs, n_steps, group, n_new, lam_init):
    del pt_ref
    k_refs, v_refs = refs[0:group], refs[group:2 * group]
    kn_ref, vn_ref, ga_ref, lam_ref, sub_ref, o_ref, s_ref, m_ref, l_ref, acc_ref = refs[2 * group:]
    step = pl.program_id(1)
    rows_h = 2 * n_new

    @pl.when(step == 0)
    def _():
        m_ref[...] = jnp.full(m_ref.shape, NEG_BIG, F32)
        l_ref[...] = jnp.zeros(l_ref.shape, F32)
        acc_ref[...] = jnp.zeros(acc_ref.shape, F32)

    def block(k_list, v_list, is_new):
        width = PAGE_SIZE * len(k_list)
        for j, k_ref in enumerate(k_list):
            for h in range(A_HEADS):
                kh = k_ref[2 * h:2 * h + 2].reshape(2 * A_HD, PAGE_SIZE).astype(BF16)
                s_ref[rows_h * h:rows_h * (h + 1), PAGE_SIZE * j:PAGE_SIZE * (j + 1)] = _nn(q_ref[h], kh)
        s = s_ref[:, 0:width]
        if is_new:
            row = lax.broadcasted_iota(jnp.int32, s.shape, 0)
            col = lax.broadcasted_iota(jnp.int32, s.shape, 1)
            s = jnp.where(col <= (row & (n_new - 1)), s, NEG_BIG)
        m_prev = m_ref[...]
        m_new = jnp.maximum(m_prev, jnp.max(s, axis=-1, keepdims=True))
        alpha = jnp.exp(m_prev - m_new)
        p = jnp.exp(s - _tile_lanes(m_new, width // LANES))
        l_ref[...] = alpha * l_ref[...] + jnp.sum(p, axis=-1, keepdims=True)
        m_ref[...] = m_new
        pb = p.astype(BF16)
        for h in range(A_HEADS):
            rows = slice(rows_h * h, rows_h * (h + 1))
            pv = None
            for j, v_ref in enumerate(v_list):
                vh = v_ref[pl.ds(h, PAGE_SIZE, stride=A_HEADS), :].astype(BF16)
                d = _nn(pb[rows, PAGE_SIZE * j:PAGE_SIZE * (j + 1)], vh)
                pv = d if pv is None else pv + d
            acc_ref[h] = alpha[rows] * acc_ref[h] + pv

    @pl.when(step < n_steps)
    def _():
        block(k_refs, v_refs, False)

    @pl.when(step == n_steps)
    def _():
        block([kn_ref], [vn_ref], True)
        lam = _lambda_value(lam_ref, lam_init)
        linv = 1.0 / l_ref[...]
        for h in range(A_HEADS):
            o = acc_ref[h] * linv[rows_h * h:rows_h * (h + 1)]
            cols = slice(LANES * h, LANES * (h + 1))
            o_ref[:, cols] = _subln_gate(o[0:n_new], o[n_new:rows_h], lam, sub_ref[...], ga_ref[:, cols], lam_init)


def _paged_call(page_table, q_bd, cache_kt, cache_v2, knew, vnew, ga, a_lambda, subln, layer, lam_init):
    bn, n_pages = page_table.shape
    nsub = 2 * A_HEADS
    n_new = q_bd.shape[2] // 2
    group = math.gcd(PAGES_PER_STEP, n_pages)
    n_steps = n_pages // group
    last = n_pages - 1
    page = lambda j: (lambda b, p, pt: (layer, pt[b, jnp.minimum(p * group + j, last)], 0, 0, 0))
    page4 = lambda j: (lambda b, p, pt: (layer, pt[b, jnp.minimum(p * group + j, last)], 0, 0))
    k_specs = [pl.BlockSpec((None, None, nsub, A_HD, PAGE_SIZE), page(j)) for j in range(group)]
    v_specs = [pl.BlockSpec((None, None, PAGE_SIZE * A_HEADS, LANES), page4(j)) for j in range(group)]
    grid_spec = pltpu.PrefetchScalarGridSpec(
        num_scalar_prefetch=1,
        grid=(bn, n_steps + 1),
        in_specs=[pl.BlockSpec((None, A_HEADS, 2 * n_new, LANES), lambda b, p, pt: (b, 0, 0, 0))] + k_specs + v_specs + [
            pl.BlockSpec((None, nsub, A_HD, PAGE_SIZE), lambda b, p, pt: (b, 0, 0, 0)),
            pl.BlockSpec((None, PAGE_SIZE * A_HEADS, LANES), lambda b, p, pt: (b, 0, 0)),
            pl.BlockSpec((None, n_new, BRANCH_W), lambda b, p, pt: (b, 0, 0)),
            pl.BlockSpec((4, A_HD), lambda b, p, pt: (0, 0)),
            pl.BlockSpec((1, LANES), lambda b, p, pt: (0, 0))],
        out_specs=pl.BlockSpec((None, n_new, BRANCH_W), lambda b, p, pt: (b, 0, 0)),
        scratch_shapes=[pltpu.VMEM((nsub * n_new, PAGE_SIZE * group), F32), pltpu.VMEM((nsub * n_new, LANES), F32),
                        pltpu.VMEM((nsub * n_new, LANES), F32), pltpu.VMEM((A_HEADS, 2 * n_new, LANES), F32)],
    )
    return pl.pallas_call(
        functools.partial(_paged_kernel, n_steps=n_steps, group=group, n_new=n_new, lam_init=lam_init),
        grid_spec=grid_spec,
        out_shape=jax.ShapeDtypeStruct((bn, n_new, BRANCH_W), F32),
        compiler_params=_cparams(("parallel", "arbitrary")),
        name="diff_attn_sample",
    )(page_table, q_bd, *([cache_kt] * group), *([cache_v2] * group), knew, vnew, ga, a_lambda, subln)


def _shift_rows(x, prev_row):
    row = lax.broadcasted_iota(jnp.int32, x.shape, 0)
    return jnp.where(row == 0, prev_row, pltpu.roll(x, 1, 0))


def _cumsum_rows(x):
    n = x.shape[0]
    row = lax.broadcasted_iota(jnp.int32, x.shape, 0)
    d = 1
    while d < n:
        x = x + jnp.where(row >= d, pltpu.roll(x, d, 0), 0.0)
        d *= 2
    return x


def _stack2(z):
    first = _first_head(z.shape)
    zero = jnp.zeros_like(z)
    return jnp.concatenate([jnp.where(first, z, zero), jnp.where(first, zero, z)], axis=0)


def _rwkv_kernel(r_ref, k_ref, v_ref, g_ref, lo_ref, sr_ref, sk_ref, sv_ref, sl_ref, st0_ref,
                 mur_ref, muk_ref, muv_ref, mul_ref, w0_ref, a0_ref, kkw_ref, kaw_ref, rk_ref, gnw_ref, gnb_ref,
                 w2_ref, a2_ref, o_ref, st_ref, pr_ref, pk_ref, pv_ref, pl_ref, s_ref,
                 la_ref, rb_ref, v2_ref, bk_ref, x_ref, t_ref, mak_ref, mr_ref, gs_ref, u_ref, *, ln, n_valid):
    c = pl.program_id(1)
    npair = B_HEADS // 2
    l2, l4 = 2 * ln, 4 * ln

    @pl.when(c == 0)
    def _():
        pr_ref[...] = sr_ref[...]
        pk_ref[...] = sk_ref[...]
        pv_ref[...] = sv_ref[...]
        pl_ref[...] = sl_ref[...]
        s_ref[...] = st0_ref[...]

    def mix(x_ref_, prev_ref, mu_ref):
        x = x_ref_[...]
        xm = x + (_shift_rows(x, prev_ref[...]) - x) * mu_ref[...]
        prev_ref[...] = x[n_valid - 1:n_valid]
        return xm

    r = mix(r_ref, pr_ref, mur_ref)
    k = mix(k_ref, pk_ref, muk_ref)
    v = mix(v_ref, pv_ref, muv_ref)
    lo = mix(lo_ref, pl_ref, mul_ref)
    w_pre = w0_ref[...] + _nn(jnp.tanh(lo).astype(BF16), w2_ref[...])
    neg = -w_pre
    softplus = jnp.maximum(neg, 0.0) + jnp.log(1.0 + jnp.exp(-jnp.abs(neg)))
    logw = -jnp.exp(-softplus - 0.5)
    a = _sigmoid(a0_ref[...] + _nn(lo.astype(BF16), a2_ref[...]))
    kk = k * kkw_ref[...]
    k2 = k * (1.0 + (a - 1.0) * kaw_ref[...])
    if n_valid < ln:
        live = lax.broadcasted_iota(jnp.int32, logw.shape, 0) < n_valid
        logw = jnp.where(live, logw, 0.0)
        kk = jnp.where(live, kk, 0.0)
        k2 = jnp.where(live, k2, 0.0)
    cum = _cumsum_rows(logw)
    e_in = jnp.exp(cum)
    e_ex = jnp.exp(cum - logw)
    e_inv = jnp.exp(-cum)
    c_last = cum[ln - 1:ln]
    e_last = jnp.exp(c_last - cum)
    decay_l = jnp.exp(c_last)

    for pr in range(npair):
        cols = slice(LANES * pr, LANES * (pr + 1))
        kkp = kk[:, cols]
        kkp = kkp * lax.rsqrt(_seg_sum(kkp * kkp) + 1e-12)
        b = kkp * a[:, cols]
        k2p = k2[:, cols]
        la_ref[pr] = jnp.concatenate([_stack2(-kkp * e_ex[:, cols]), _stack2(r[:, cols] * e_in[:, cols])], 0).astype(BF16)
        rb_ref[pr] = jnp.concatenate([_stack2(b * e_inv[:, cols]), _stack2(k2p * e_inv[:, cols])], 0).astype(BF16)
        bk_ref[pr] = jnp.concatenate([_stack2(b * e_last[:, cols]), _stack2(k2p * e_last[:, cols])], 0).astype(BF16)
        v2_ref[pr] = _stack2(v[:, cols]).astype(BF16)

    rr = lax.broadcasted_iota(jnp.int32, (l2, l4), 0) & (ln - 1)
    cc_full = lax.broadcasted_iota(jnp.int32, (l2, l4), 1)
    cc = cc_full & (ln - 1)
    strict_right = (cc < rr) & (cc_full >= l2)
    incl = cc <= rr
    rs = lax.broadcasted_iota(jnp.int32, (l2, l2), 0)
    cs = lax.broadcasted_iota(jnp.int32, (l2, l2), 1)
    strict_sq = (cs & (ln - 1)) < (rs & (ln - 1))
    eye = (rs == cs).astype(F32)
    for pr in range(npair):
        la = la_ref[pr]
        mm = _nt(la, rb_ref[pr])
        xa = mm[0:l2]
        x = jnp.where(strict_sq, xa[:, 0:l2], 0.0)
        x_ref[pr] = x
        t_ref[pr] = eye + x
        mak_ref[pr] = jnp.where(strict_right, xa, 0.0).astype(BF16)
        mr_ref[pr] = jnp.where(incl, mm[l2:l4], 0.0).astype(BF16)
        gs_ref[pr] = _nt(la, s_ref[pr].astype(BF16))
    for pr in range(npair):
        v2 = v2_ref[pr]
        u_ref[pr] = gs_ref[pr, 0:l2, :] + _nn(mak_ref[pr], jnp.concatenate([v2, v2], axis=0))

    n = 1
    while 2 * n < ln:
        for pr in range(npair):
            xb = x_ref[pr].astype(BF16)
            x_ref[pr] = _nn(xb, xb)
        for pr in range(npair):
            t = t_ref[pr]
            t_ref[pr] = t + _nn(t.astype(BF16), x_ref[pr].astype(BF16))
        n *= 2

    for pr in range(npair):
        u_ref[pr] = _nn(t_ref[pr].astype(BF16), u_ref[pr].astype(BF16))

    rk_all = r * k2 * rk_ref[...]
    gate = _silu(g_ref[...])
    for pr in range(npair):
        cols = slice(LANES * pr, LANES * (pr + 1))
        u2 = u_ref[pr]
        v2 = v2_ref[pr]
        y2 = gs_ref[pr, l2:l4, :] + _nn(mr_ref[pr], jnp.concatenate([u2.astype(BF16), v2], axis=0))
        y = y2[0:ln] + y2[ln:l2]
        uvt = jnp.concatenate([u2, v2.astype(F32)], axis=0).T.astype(BF16)
        s_ref[pr] = s_ref[pr] * decay_l[:, cols] + _nn(uvt, bk_ref[pr])
        mu = _seg_sum(y) * (1.0 / HALF)
        dy = y - mu
        var = _seg_sum(dy * dy) * (1.0 / HALF)
        yn = dy * lax.rsqrt(var + B_GN_EPS) * gnw_ref[:, cols] + gnb_ref[:, cols]
        yn = yn + _seg_sum(rk_all[:, cols]) * v[:, cols]
        o_ref[:, cols] = (yn * gate[:, cols]).astype(BF16)

    @pl.when(c == pl.num_programs(1) - 1)
    def _():
        st_ref[...] = s_ref[...]


def _rwkv_call(p, lora, shift_r, shift_k, shift_v, shift_l, state_bd, prm, bn, s, ln, n_valid):
    nc = s // ln
    npair = B_HEADS // 2
    rb, kb, vb, gb = (c // BRANCH_W for c in (COL_BR, COL_BK, COL_BV, COL_BG))
    tok = lambda cb: pl.BlockSpec((ln, BRANCH_W), lambda b, c: (b * nc + c, cb))
    per_b = lambda w: pl.BlockSpec((None, 1, w), lambda b, c: (b, 0, 0))
    row = lambda w: pl.BlockSpec((1, w), lambda b, c: (0, 0))
    st_spec = pl.BlockSpec((None, npair, LANES, LANES), lambda b, c: (b, 0, 0, 0))
    l2, l4 = 2 * ln, 4 * ln
    return pl.pallas_call(
        functools.partial(_rwkv_kernel, ln=ln, n_valid=n_valid),
        grid=(bn, nc),
        in_specs=[tok(rb), tok(kb), tok(vb), tok(gb),
                  pl.BlockSpec((ln, LANES), lambda b, c: (b * nc + c, 0)),
                  per_b(BRANCH_W), per_b(BRANCH_W), per_b(BRANCH_W), per_b(LANES), st_spec,
                  row(BRANCH_W), row(BRANCH_W), row(BRANCH_W), row(LANES),
                  row(BRANCH_W), row(BRANCH_W), row(BRANCH_W), row(BRANCH_W), row(BRANCH_W), row(BRANCH_W),
                  row(BRANCH_W),
                  pl.BlockSpec((LANES, BRANCH_W), lambda b, c: (0, 0)),
                  pl.BlockSpec((LANES, BRANCH_W), lambda b, c: (0, 0))],
        out_specs=[pl.BlockSpec((ln, BRANCH_W), lambda b, c: (b * nc + c, 0)), st_spec],
        out_shape=[jax.ShapeDtypeStruct((bn * s, BRANCH_W), BF16),
                   jax.ShapeDtypeStruct((bn, npair, LANES, LANES), F32)],
        scratch_shapes=[pltpu.VMEM((1, BRANCH_W), F32), pltpu.VMEM((1, BRANCH_W), F32),
                        pltpu.VMEM((1, BRANCH_W), F32), pltpu.VMEM((1, LANES), F32),
                        pltpu.VMEM((npair, LANES, LANES), F32),
                        pltpu.VMEM((npair, l4, LANES), BF16), pltpu.VMEM((npair, l4, LANES), BF16),
                        pltpu.VMEM((npair, l2, LANES), BF16), pltpu.VMEM((npair, l4, LANES), BF16),
                        pltpu.VMEM((npair, l2, l2), F32), pltpu.VMEM((npair, l2, l2), F32),
                        pltpu.VMEM((npair, l2, l4), BF16), pltpu.VMEM((npair, l2, l4), BF16),
                        pltpu.VMEM((npair, l4, LANES), F32), pltpu.VMEM((npair, l2, LANES), F32)],
        compiler_params=_cparams(("parallel", "arbitrary")),
        name="rwkv7_chunked",
    )(p, p, p, p, lora, shift_r, shift_k, shift_v, shift_l, state_bd,
      prm["mu_r"], prm["mu_k"], prm["mu_v"], prm["mu_l"], prm["w0"], prm["a0"], prm["kk"], prm["ka"], prm["rk"],
      prm["gn_w"], prm["gn_b"], prm["w2"], prm["a2"])


def _ret_kernel(q_ref, k_ref, v_ref, g_ref, cos_ref, sin_ref, st0_ref, o_ref, st_ref, s_ref, *, cn, n_valid):
    c = pl.program_id(1)

    @pl.when(c == 0)
    def _():
        s_ref[...] = st0_ref[...]

    cos = cos_ref[...]
    sin = sin_ref[...]
    rowc = lax.broadcasted_iota(jnp.int32, (cn, cn), 0)
    colc = lax.broadcasted_iota(jnp.int32, (cn, cn), 1)
    dist = (rowc - colc).astype(F32)
    causal = colc <= rowc
    n_idx = lax.broadcasted_iota(jnp.int32, (cn, 1), 0).astype(F32)
    first128 = _first_head((1, LANES))
    first256 = lax.broadcasted_iota(jnp.int32, (1, 2 * LANES), 1) < LANES
    ri = lax.broadcasted_iota(jnp.int32, (LANES, 2 * LANES), 0) < HALF
    ci = lax.broadcasted_iota(jnp.int32, (LANES, 2 * LANES), 1) < LANES
    live = lax.broadcasted_iota(jnp.int32, (cn, 1), 0) < n_valid
    for pr in range(C_HEADS // 2):
        lg = [math.log(1.0 - 2.0 ** (-5.0 - (2 * pr + j))) for j in range(2)]
        qc = slice(LANES * pr, LANES * (pr + 1))
        vc = slice(2 * LANES * pr, 2 * LANES * (pr + 1))
        q = _rope(q_ref[:, qc], cos, sin)
        k = _rope(k_ref[:, qc], cos, sin) * (C_KD ** -0.5)
        if n_valid < cn:
            k = jnp.where(live, k, 0.0)
        v = v_ref[:, vc]
        vb = v.astype(BF16)
        kb = k.astype(BF16)
        zero = jnp.zeros_like(q)
        outs = []
        for j in range(2):
            qj = jnp.where(first128, q, zero) if j == 0 else jnp.where(first128, zero, q)
            sc = _nt(qj.astype(BF16), kb)
            sc = sc * jnp.where(causal, jnp.exp(jnp.maximum(dist, 0.0) * lg[j]), 0.0)
            outs.append(_nn(sc.astype(BF16), vb[:, LANES * j:LANES * (j + 1)]))
        st = s_ref[pr]
        xi = jnp.where(first256, jnp.exp((n_idx + 1.0) * lg[0]), jnp.exp((n_idx + 1.0) * lg[1]))
        o = jnp.concatenate(outs, axis=1) + _nn(q.astype(BF16), st.astype(BF16)) * xi
        zeta = jnp.where(first128, jnp.exp((n_valid - 1.0 - n_idx) * lg[0]), jnp.exp((n_valid - 1.0 - n_idx) * lg[1]))
        upd = _nn((k * zeta).T.astype(BF16), vb)
        gch = jnp.where(first256, math.exp(n_valid * lg[0]), math.exp(n_valid * lg[1]))
        s_ref[pr] = st * gch + jnp.where(ri == ci, upd, 0.0)
        gate = _silu(g_ref[:, vc])
        for j in range(2):
            oj = o[:, LANES * j:LANES * (j + 1)]
            oj = oj * lax.rsqrt(jnp.mean(oj * oj, axis=-1, keepdims=True) + NORM_EPS)
            o_ref[:, 2 * LANES * pr + LANES * j:2 * LANES * pr + LANES * (j + 1)] = (
                oj * gate[:, LANES * j:LANES * (j + 1)]).astype(BF16)

    @pl.when(c == pl.num_programs(1) - 1)
    def _():
        st_ref[...] = s_ref[...]


def _ret_call(p, cos, sin, state_bd, bn, s, cn, n_valid):
    nc = s // cn
    npair = C_HEADS // 2
    ntab = cos.shape[0] // cn
    st_spec = pl.BlockSpec((None, npair, LANES, 2 * LANES), lambda b, c: (b, 0, 0, 0))
    return pl.pallas_call(
        functools.partial(_ret_kernel, cn=cn, n_valid=n_valid),
        grid=(bn, nc),
        in_specs=[pl.BlockSpec((cn, 512), lambda b, c: (b * nc + c, COL_CQ // 512)),
                  pl.BlockSpec((cn, 512), lambda b, c: (b * nc + c, COL_CK // 512)),
                  pl.BlockSpec((cn, BRANCH_W), lambda b, c: (b * nc + c, COL_CV // BRANCH_W)),
                  pl.BlockSpec((cn, BRANCH_W), lambda b, c: (b * nc + c, COL_CG // BRANCH_W)),
                  pl.BlockSpec((cn, LANES), lambda b, c: (c % ntab, 0)),
                  pl.BlockSpec((cn, LANES), lambda b, c: (c % ntab, 0)),
                  st_spec],
        out_specs=[pl.BlockSpec((cn, BRANCH_W), lambda b, c: (b * nc + c, 0)), st_spec],
        out_shape=[jax.ShapeDtypeStruct((bn * s, BRANCH_W), BF16),
                   jax.ShapeDtypeStruct((bn, npair, LANES, 2 * LANES), F32)],
        scratch_shapes=[pltpu.VMEM((npair, LANES, 2 * LANES), F32)],
        compiler_params=_cparams(("parallel", "arbitrary")),
        name="retention_chunkwise",
    )(p, p, p, p, cos, sin, state_bd)


def _merge_kernel(oa_ref, ob_ref, oc_ref, wb_ref, g0_ref, g1_ref, g2_ref, o_ref):
    acc = _sigmoid(g0_ref[...]) * _nn(oa_ref[...], wb_ref[0])
    acc = acc + _sigmoid(g1_ref[...]) * _nn(ob_ref[...], wb_ref[1])
    acc = acc + _sigmoid(g2_ref[...]) * _nn(oc_ref[...], wb_ref[2])
    o_ref[...] = acc.astype(BF16)


def _merge_call(oa, ob, oc, wb, layer, p, tm, tn):
    m = oa.shape[0]
    nj = D_MODEL // tn
    assert COL_GATE % tn == 0 and D_MODEL % tn == 0
    act = pl.BlockSpec((tm, BRANCH_W), lambda i, j: (i, 0))
    gate = lambda n: pl.BlockSpec((tm, tn), lambda i, j: (i, (COL_GATE + n * D_MODEL) // tn + j))
    return pl.pallas_call(
        _merge_kernel,
        grid=(m // tm, nj),
        in_specs=[act, act, act, pl.BlockSpec((None, 3, BRANCH_W, tn), lambda i, j: (layer, 0, 0, j)), gate(0), gate(1), gate(2)],
        out_specs=pl.BlockSpec((tm, tn), lambda i, j: (i, j)),
        out_shape=jax.ShapeDtypeStruct((m, D_MODEL), BF16),
        compiler_params=_cparams(("parallel", "arbitrary")),
        name="branch_merge",
    )(oa, ob, oc, wb, p, p, p)


def _out_kernel(a_ref, b_ref, x_ref, o_ref):
    o_ref[...] = x_ref[...] + _nn(a_ref[...], b_ref[...])


def _out_call(merged, w_out, layer, x2d, tm, tn):
    m = merged.shape[0]
    return pl.pallas_call(
        _out_kernel,
        grid=(m // tm, D_MODEL // tn),
        in_specs=[pl.BlockSpec((tm, D_MODEL), lambda i, j: (i, 0)), pl.BlockSpec((None, D_MODEL, tn), lambda i, j: (layer, 0, j)),
                  pl.BlockSpec((tm, tn), lambda i, j: (i, j))],
        out_specs=pl.BlockSpec((tm, tn), lambda i, j: (i, j)),
        out_shape=jax.ShapeDtypeStruct((m, D_MODEL), F32),
        compiler_params=_cparams(("parallel", "arbitrary")),
        name="out_proj_residual",
    )(merged, w_out, x2d)


def _rope_tables(pos, inv_freq):
    ang = pos.astype(F32)[:, None] * inv_freq[None, :]
    cos = jnp.cos(ang)
    sin = jnp.sin(ang)
    cos = jnp.concatenate([cos, cos, cos, cos], axis=1)
    sin = jnp.concatenate([-sin, sin, -sin, sin], axis=1)
    return cos, sin


def _permute_w_in(w):
    b0 = 4096
    c0 = b0 + B_SHIFT + BRANCH_W
    cut = lambda lo, hi: w[..., lo:hi]
    main = jnp.concatenate([cut(0, b0 + 1024), cut(b0 + 1088, b0 + 3136), cut(b0 + 3200, b0 + 4224),
                            cut(c0, c0 + 3072 + 3 * D_MODEL)], axis=-1)
    lora = jnp.concatenate([cut(b0 + 1024, b0 + 1088), cut(b0 + 3136, b0 + 3200)], axis=-1)
    return main.astype(BF16), lora.astype(BF16)


def _split_shift(sh):
    return (sh[..., 0:1024], sh[..., 1088:2112], sh[..., 2112:3136],
            jnp.concatenate([sh[..., 1024:1088], sh[..., 3136:3200]], axis=-1))


def _join_shift(r, k, v, lo):
    return jnp.concatenate([r, lo[..., 0:64], k, v, lo[..., 64:128]], axis=-1)


def _pairs_to_blockdiag(st):
    bn, nh, r, c = st.shape
    st = st.reshape(bn, nh // 2, 2, r, c)
    z = jnp.zeros_like(st[:, :, 0])
    top = jnp.concatenate([st[:, :, 0], z], axis=-1)
    bot = jnp.concatenate([z, st[:, :, 1]], axis=-1)
    return jnp.concatenate([top, bot], axis=-2)


def _blockdiag_to_pairs(bd):
    bn, npair, r2, c2 = bd.shape
    r, c = r2 // 2, c2 // 2
    return jnp.stack([bd[:, :, :r, :c], bd[:, :, r:, c:]], axis=2).reshape(bn, 2 * npair, r, c)


def _layer_params(l, w_main, w_lora, w_branch_bf, w_out_bf, norm_w, a_qnorm, a_knorm, a_lambda, a_subln, b_mu, b_w0,
                  b_w2, b_a0, b_a2, b_kk, b_ka, b_rk, b_gn_w, b_gn_b):
    mu_r, mu_k, mu_v, mu_l = _split_shift(b_mu[l])
    zeros = jnp.zeros((B_LORA, BRANCH_W), F32)
    row = lambda t: t.reshape(1, -1)
    return dict(
        layer=l, w_main=w_main, w_lora=w_lora, w_branch=w_branch_bf, w_out=w_out_bf, norm_w=row(norm_w[l]),
        qn=row(jnp.tile(a_qnorm[l], 2)), kn=row(jnp.tile(a_knorm[l], 2)), a_lambda=a_lambda[l], subln=row(a_subln[l]),
        lam_init=0.8 - 0.6 * math.exp(-0.3 * l),
        rwkv=dict(mu_r=row(mu_r), mu_k=row(mu_k), mu_v=row(mu_v), mu_l=row(mu_l), w0=row(b_w0[l]), a0=row(b_a0[l]),
                  kk=row(b_kk[l]), ka=row(b_ka[l]), rk=row(b_rk[l]), gn_w=row(b_gn_w[l]), gn_b=row(b_gn_b[l]),
                  w2=jnp.concatenate([b_w2[l], zeros], axis=0).astype(BF16),
                  a2=jnp.concatenate([zeros, b_a2[l]], axis=0).astype(BF16)))


def _tiles(m):
    return min(m, 1024)


def _trunk_layer(x2d, bn, s, n_valid, lp, tabs_a, tabs_c, attend, shift0, rwkv0_bd, ret0_bd, chunk_b, chunk_c):
    m = bn * s
    tm = _tiles(m)
    h = _rms_call(x2d, lp["norm_w"], min(m, 512))
    small = m <= LANES
    p = _mm_call(h, lp["w_main"], lp["layer"], tm, N_MAIN // 4 if small else 1024, "in_proj")
    lora = _mm_call(h, lp["w_lora"], lp["layer"], tm, LANES, "in_proj_lora")
    t_qk = min(tabs_a[0].shape[0], 512)
    q_bf, k_f32, k_bf = _qk_call(p, lp["qn"], lp["kn"], tabs_a[0], tabs_a[1], t_qk)
    oa = attend(q_bf, k_f32, k_bf, p)
    sr, sk, sv, sl = shift0
    ob, rwkv_bd = _rwkv_call(p, lora, sr, sk, sv, sl, rwkv0_bd, lp["rwkv"], bn, s, chunk_b, min(n_valid, chunk_b))
    oc, ret_bd = _ret_call(p, tabs_c[0], tabs_c[1], ret0_bd, bn, s, chunk_c, min(n_valid, chunk_c))
    merged = _merge_call(oa, ob, oc, lp["w_branch"], lp["layer"], p, min(m, 512), 1024)
    y = _out_call(merged, lp["w_out"], lp["layer"], x2d, tm, D_MODEL if small else 1024)
    return y, k_f32, p, lora, rwkv_bd, ret_bd


def kernel(x_prompt, x_sample, cache_k, cache_v, page_table, state_rwkv, state_shift, state_ret, norm_w, w_in, a_qnorm, a_knorm, a_lambda, a_subln, b_mu, b_w0, b_w2, b_a0, b_a2, b_kk, b_ka, b_rk, b_gn_w, b_gn_b, w_branch, w_out):
    bp, sp, _ = x_prompt.shape
    bs, ss, _ = x_sample.shape
    ss_pad = 16
    depth = w_in.shape[0]
    n_pool = cache_k.shape[1]

    w_main, w_lora = _permute_w_in(w_in)
    w_branch_bf = w_branch.astype(BF16)
    w_out_bf = w_out.astype(BF16)
    cache_kt = jnp.transpose(cache_k, (0, 1, 3, 4, 2))
    cache_v2 = cache_v.reshape(depth, n_pool, PAGE_SIZE * A_HEADS, LANES)

    inv_a = ROPE_THETA ** (-jnp.arange(A_HD // 2, dtype=F32) / (A_HD // 2))
    inv_c = ROPE_THETA ** (-jnp.linspace(0.0, 1.0, C_KD // 2, dtype=F32))
    pos_p = jnp.arange(sp)
    pos_s = PAST_LEN + jnp.arange(ss_pad)
    tabs_a_p, tabs_c_p = _rope_tables(pos_p, inv_a), _rope_tables(pos_p, inv_c)
    tabs_a_s = tuple(jnp.tile(t, (bs, 1)) for t in _rope_tables(pos_s, inv_a))
    tabs_c_s = _rope_tables(pos_s, inv_c)

    xp = x_prompt.reshape(bp * sp, D_MODEL)
    xs = jnp.pad(x_sample, ((0, 0), (0, ss_pad - ss), (0, 0))).reshape(bs * ss_pad, D_MODEL)
    zero_shift = (jnp.zeros((bp, 1, BRANCH_W), F32),) * 3 + (jnp.zeros((bp, 1, LANES), F32),)
    rwkv0_p = jnp.zeros((bp, B_HEADS // 2, LANES, LANES), F32)
    ret0_p = jnp.zeros((bp, C_HEADS // 2, LANES, 2 * LANES), F32)

    outs = [[] for _ in range(10)]
    for l in range(depth):
        lp = _layer_params(l, w_main, w_lora, w_branch_bf, w_out_bf, norm_w, a_qnorm, a_knorm, a_lambda, a_subln, b_mu,
                           b_w0, b_w2, b_a0, b_a2, b_kk, b_ka, b_rk, b_gn_w, b_gn_b)

        def attend_p(q_bf, k_f32, k_bf, p, lp=lp):
            return _flash_call(q_bf, k_bf, p, lp["a_lambda"], lp["subln"], bp, sp, min(sp, 512), lp["lam_init"])

        xp, k_new, p, lora, rw, rt = _trunk_layer(xp, bp, sp, sp, lp, tabs_a_p, tabs_c_p, attend_p, zero_shift,
                                                  rwkv0_p, ret0_p, 64, 128)
        last = p.reshape(bp, sp, N_MAIN)[:, sp - 1]
        outs[0].append(k_new.reshape(bp, sp, 2 * A_HEADS, A_HD))
        outs[1].append(p.reshape(bp, sp, N_MAIN)[:, :, COL_AV:COL_AV + BRANCH_W].reshape(bp, sp, A_HEADS, LANES))
        outs[2].append(_blockdiag_to_pairs(rw))
        outs[3].append(_join_shift(last[:, COL_BR:COL_BR + 1024], last[:, COL_BK:COL_BK + 1024],
                                   last[:, COL_BV:COL_BV + 1024], lora.reshape(bp, sp, LANES)[:, sp - 1]))
        outs[4].append(_blockdiag_to_pairs(rt))

        def attend_s(q_bf, k_f32, k_bf, p, lp=lp, l=l):
            q5 = q_bf.reshape(bs, ss_pad, A_HEADS, 2, A_HD)[:, :ss].transpose(0, 2, 3, 1, 4)
            zq = jnp.zeros_like(q5[:, :, 0])
            q_bd = jnp.concatenate([jnp.concatenate([q5[:, :, 0], zq], axis=-1),
                                    jnp.concatenate([zq, q5[:, :, 1]], axis=-1)], axis=2)
            kn = k_f32.reshape(bs, ss_pad, 2 * A_HEADS, A_HD)[:, :ss].transpose(0, 2, 3, 1)
            kn = jnp.pad(kn, ((0, 0), (0, 0), (0, 0), (0, PAGE_SIZE - ss)))
            p3 = p.reshape(bs, ss_pad, N_MAIN)[:, :ss]
            vn = p3[:, :, COL_AV:COL_AV + BRANCH_W].reshape(bs, ss, A_HEADS, LANES)
            vn = jnp.pad(vn, ((0, 0), (0, PAGE_SIZE - ss), (0, 0), (0, 0))).reshape(bs, PAGE_SIZE * A_HEADS, LANES)
            ga = p3[:, :, COL_AG:COL_AG + BRANCH_W]
            o = _paged_call(page_table, q_bd, cache_kt, cache_v2, kn, vn, ga, lp["a_lambda"], lp["subln"], l,
                            lp["lam_init"])
            return jnp.pad(o, ((0, 0), (0, ss_pad - ss), (0, 0))).astype(BF16).reshape(bs * ss_pad, BRANCH_W)

        sh = tuple(t[:, None, :] for t in _split_shift(state_shift[l]))
        rwkv0_s = _pairs_to_blockdiag(state_rwkv[l])
        ret0_s = _pairs_to_blockdiag(state_ret[l])
        xs, k_new, p, lora, rw, rt = _trunk_layer(xs, bs, ss_pad, ss, lp, tabs_a_s, tabs_c_s, attend_s, sh,
                                                  rwkv0_s, ret0_s, ss_pad, ss_pad)
        p3 = p.reshape(bs, ss_pad, N_MAIN)
        last = p3[:, ss - 1]
        outs[5].append(k_new.reshape(bs, ss_pad, 2 * A_HEADS, A_HD)[:, :ss])
        outs[6].append(p3[:, :ss, COL_AV:COL_AV + BRANCH_W].reshape(bs, ss, A_HEADS, LANES))
        outs[7].append(_blockdiag_to_pairs(rw))
        outs[8].append(_join_shift(last[:, COL_BR:COL_BR + 1024], last[:, COL_BK:COL_BK + 1024],
                                   last[:, COL_BV:COL_BV + 1024], lora.reshape(bs, ss_pad, LANES)[:, ss - 1]))
        outs[9].append(_blockdiag_to_pairs(rt))

    st = [jnp.stack(o) for o in outs]
    y_prompt = xp.reshape(bp, sp, D_MODEL)
    y_sample = xs.reshape(bs, ss_pad, D_MODEL)[:, :ss]
    return (y_prompt, y_sample, st[0], st[1], st[2], st[3], st[4], st[5], st[6], st[7], st[8], st[9])
```
